```python
import math
import jax, jax.numpy as jnp
from jax import lax
import numpy as np

D_MODEL = 1024
BATCH = 8
SEQ = 2048
DEPTH = 1
DEC_BATCH = 8
DEC_SEQ = 4096
PAST_LEN = 128

MIX_W = D_MODEL
ATTN_W = MIX_W // 2
HEAD_DIM = 64
N_Q_HEADS = ATTN_W // HEAD_DIM
N_KV_HEADS = 2
GQA_GROUP = N_Q_HEADS // N_KV_HEADS
KV_W = N_KV_HEADS * HEAD_DIM
DN_W = MIX_W - ATTN_W
DN_HEAD_DIM = 128
N_DN_HEADS = DN_W // DN_HEAD_DIM
CONV_W = 5
CHUNK = 64
Q_BLOCK = 128
GRID_W = 64
ROPE_THETA = 10000.0
N_EXPERTS = 32
TOP_K = 4
D_FF = D_MODEL
SWIGLU_LIMIT = 7.0
SWIGLU_ALPHA = 1.702
MOE_BLOCK = 128
NORM_EPS = 1e-6
PROJ_W = ATTN_W + 2 * KV_W + 4 * DN_W + 4 * N_DN_HEADS

kernel_name = "hybrid_bidir_attn_deltanet_moe"


def rms_norm(x, w):
    xf = x.astype(jnp.float32)
    y = xf * lax.rsqrt(jnp.mean(xf * xf, axis=-1, keepdims=True) + NORM_EPS)
    return (y * w.astype(jnp.float32)).astype(x.dtype)


def l2_normalize(x):
    return x * lax.rsqrt(jnp.sum(x * x, axis=-1, keepdims=True) + NORM_EPS)


def axial_rope(x):
    T, dh = x.shape[1], x.shape[-1]
    rows = T // GRID_W
    row = jnp.repeat(jnp.arange(rows), GRID_W).astype(jnp.float32)
    col = jnp.tile(jnp.arange(GRID_W), rows).astype(jnp.float32)
    half = dh // 2
    freqs = ROPE_THETA ** (-jnp.arange(0, half, 2, dtype=jnp.float32) / half)

    def rotate(xh, pos):
        ang = pos[:, None] * freqs[None, :]
        cos = jnp.concatenate([jnp.cos(ang)] * 2, axis=-1)[None, :, None, :]
        sin = jnp.concatenate([jnp.sin(ang)] * 2, axis=-1)[None, :, None, :]
        x1, x2 = jnp.split(xh, 2, axis=-1)
        return xh * cos + jnp.concatenate([-x2, x1], axis=-1) * sin

    xf = x.astype(jnp.float32)
    out = jnp.concatenate([rotate(xf[..., :half], row), rotate(xf[..., half:], col)], axis=-1)
    return out.astype(x.dtype)


def block_attention(q, k, v):
    B, T = q.shape[0], q.shape[1]
    nb = T // Q_BLOCK
    qb = q.reshape(B, nb, Q_BLOCK, N_KV_HEADS, GQA_GROUP, HEAD_DIM).transpose(1, 0, 2, 3, 4, 5)
    scale = HEAD_DIM ** -0.5

    def one_block(qc):
        s = jnp.einsum('bqkgd,bskd->bkgqs', qc, k, preferred_element_type=jnp.float32) * scale
        p = jax.nn.softmax(s, axis=-1).astype(v.dtype)
        return jnp.einsum('bkgqs,bskd->bqkgd', p, v)

    o = lax.map(one_block, qb)
    return o.transpose(1, 0, 2, 3, 4, 5).reshape(B, T, ATTN_W)


def gated_delta_chunked(q, k, v, g, beta):
    B, H, T, dk = k.shape
    dv = v.shape[-1]
    n = T // CHUNK
    q = q.reshape(B, H, n, CHUNK, dk)
    k = k.reshape(B, H, n, CHUNK, dk)
    v = v.reshape(B, H, n, CHUNK, dv)
    g = jnp.cumsum(g.reshape(B, H, n, CHUNK), axis=-1)
    beta = beta.reshape(B, H, n, CHUNK)[..., None]
    k_beta = k * beta
    v_beta = v * beta
    lower = jnp.tril(jnp.ones((CHUNK, CHUNK), dtype=bool))
    strict = jnp.tril(jnp.ones((CHUNK, CHUNK), dtype=bool), -1)
    decay = jnp.exp(jnp.where(lower, g[..., :, None] - g[..., None, :], -jnp.inf))
    a = jnp.where(strict, jnp.einsum('bhncd,bhnsd->bhncs', k_beta, k) * decay, 0.0)
    rhs = jnp.concatenate([v_beta, k_beta * jnp.exp(g)[..., None]], axis=-1)
    sol = lax.linalg.triangular_solve(a + jnp.eye(CHUNK, dtype=a.dtype), rhs,
                                      left_side=True, lower=True, unit_diagonal=True)
    u, w = sol[..., :dv], sol[..., dv:]
    qk = jnp.einsum('bhncd,bhnsd->bhncs', q, k) * decay
    q_dec = q * jnp.exp(g)[..., None]
    k_dec = k * jnp.exp(g[..., -1:] - g)[..., None]
    g_tot = jnp.exp(g[..., -1])[..., None, None]

    def step(S, inp):
        qk_c, u_c, w_c, qd_c, kd_c, gt_c = inp
        v_new = u_c - jnp.einsum('bhcd,bhde->bhce', w_c, S)
        o = jnp.einsum('bhcd,bhde->bhce', qd_c, S) + jnp.einsum('bhcs,bhse->bhce', qk_c, v_new)
        S = S * gt_c + jnp.einsum('bhcd,bhce->bhde', kd_c, v_new)
        return S, o

    xs = tuple(jnp.moveaxis(t, 2, 0) for t in (qk, u, w, q_dec, k_dec, g_tot))
    _, o = lax.scan(step, jnp.zeros((B, H, dk, dv), jnp.float32), xs)
    return jnp.moveaxis(o, 0, 2).reshape(B, H, T, dv)


def bidirectional_gated_deltanet(qkv, z, b, a, conv_w, A_log, dt_bias, dn_norm_w):
    B, T, C = qkv.shape
    qkv = lax.conv_general_dilated(qkv, conv_w.astype(qkv.dtype)[:, None, :], (1,),
                                   [(CONV_W // 2, CONV_W // 2)],
                                   dimension_numbers=('NWC', 'WIO', 'NWC'),
                                   feature_group_count=C)
    qkv = jax.nn.silu(qkv).astype(jnp.float32)
    q, k, v = jnp.split(qkv, 3, axis=-1)

    def heads(t):
        return t.reshape(B, T, N_DN_HEADS, DN_HEAD_DIM).transpose(0, 2, 1, 3)

    q = l2_normalize(heads(q)) * (DN_HEAD_DIM ** -0.5)
    k = l2_normalize(heads(k))
    v = heads(v)
    beta = jax.nn.sigmoid(b.astype(jnp.float32)).reshape(B, T, 2, N_DN_HEADS).transpose(2, 0, 3, 1)
    a_dir = a.astype(jnp.float32).reshape(B, T, 2, N_DN_HEADS).transpose(2, 0, 3, 1)
    g = -jnp.exp(A_log.astype(jnp.float32))[:, None, :, None] * jax.nn.softplus(
        a_dir + dt_bias.astype(jnp.float32)[:, None, :, None])

    def flip(t):
        return jnp.flip(t, axis=2)

    o_fwd = gated_delta_chunked(q, k, v, g[0], beta[0])
    o_bwd = flip(gated_delta_chunked(flip(q), flip(k), flip(v), flip(g[1]), flip(beta[1])))
    o = (o_fwd + o_bwd).transpose(0, 2, 1, 3)
    gate = jax.nn.silu(z.astype(jnp.float32)).reshape(B, T, N_DN_HEADS, DN_HEAD_DIM)
    o = rms_norm(o, dn_norm_w) * gate
    return o.reshape(B, T, DN_W).astype(z.dtype)


def routed_experts(x2d, router_w, router_b, w1, b1, w2, b2):
    n_tok, d = x2d.shape
    logits = (x2d @ router_w + router_b).astype(jnp.float32)
    top_v, top_i = lax.top_k(logits, TOP_K)
    gates = jax.nn.softmax(top_v, axis=-1)
    n_assign = n_tok * TOP_K
    flat_e = top_i.reshape(-1)
    flat_tok = jnp.repeat(jnp.arange(n_tok, dtype=jnp.int32), TOP_K)
    flat_g = gates.reshape(-1)
    order = jnp.argsort(flat_e)
    sorted_e = flat_e[order]
    counts = jnp.bincount(flat_e, length=N_EXPERTS)
    padded = (counts + MOE_BLOCK - 1) // MOE_BLOCK * MOE_BLOCK
    start = jnp.cumsum(counts) - counts
    pad_end = jnp.cumsum(padded)
    pad_start = pad_end - padded
    dest = pad_start[sorted_e] + jnp.arange(n_assign, dtype=jnp.int32) - start[sorted_e]
    n_rows = (n_assign + N_EXPERTS * (MOE_BLOCK - 1) + MOE_BLOCK - 1) // MOE_BLOCK * MOE_BLOCK
    n_blocks = n_rows // MOE_BLOCK
    row_tok = jnp.zeros((n_rows,), jnp.int32).at[dest].set(flat_tok[order])
    row_gate = jnp.zeros((n_rows,), jnp.float32).at[dest].set(flat_g[order])
    block_e = jnp.minimum(jnp.searchsorted(pad_end, jnp.arange(n_blocks, dtype=jnp.int32) * MOE_BLOCK,
                                           side='right'), N_EXPERTS - 1)
    xs = x2d[row_tok].reshape(n_blocks, MOE_BLOCK, d)

    def expert_block(args):
        xb, e = args
        h = xb @ w1[e] + b1[e]
        gate = jnp.minimum(h[:, :D_FF], SWIGLU_LIMIT)
        up = jnp.clip(h[:, D_FF:], -SWIGLU_LIMIT, SWIGLU_LIMIT)
        act = gate * jax.nn.sigmoid(SWIGLU_ALPHA * gate) * (up + 1.0)
        return act @ w2[e] + b2[e]

    out = lax.map(expert_block, (xs, block_e)).reshape(n_rows, d)
    return jax.ops.segment_sum(out * row_gate[:, None].astype(out.dtype), row_tok, num_segments=n_tok)


def encoder_layer(x, norm1_w, w_in, conv_w, q_norm_w, k_norm_w, A_log, dt_bias, dn_norm_w,
                  w_out, norm2_w, router_w, router_b, w1, b1, w2, b2):
    B, T, D = x.shape
    h = rms_norm(x, norm1_w)
    proj = h @ w_in
    i0 = ATTN_W
    i1 = i0 + KV_W
    i2 = i1 + KV_W
    i3 = i2 + 3 * DN_W
    i4 = i3 + DN_W
    i5 = i4 + 2 * N_DN_HEADS
    q_a, k_a, v_a = proj[..., :i0], proj[..., i0:i1], proj[..., i1:i2]
    qkv_dn, z_dn, b_dn, a_dn = proj[..., i2:i3], proj[..., i3:i4], proj[..., i4:i5], proj[..., i5:]
    q = axial_rope(rms_norm(q_a.reshape(B, T, N_Q_HEADS, HEAD_DIM), q_norm_w))
    k = axial_rope(rms_norm(k_a.reshape(B, T, N_KV_HEADS, HEAD_DIM), k_norm_w))
    v = v_a.reshape(B, T, N_KV_HEADS, HEAD_DIM)
    attn = block_attention(q, k, v)
    dn = bidirectional_gated_deltanet(qkv_dn, z_dn, b_dn, a_dn, conv_w, A_log, dt_bias, dn_norm_w)
    x = x + jnp.concatenate([attn, dn], axis=-1) @ w_out
    h2 = rms_norm(x, norm2_w)
    x = x + routed_experts(h2.reshape(B * T, D), router_w, router_b, w1, b1, w2, b2).reshape(B, T, D)
    return x


def run_trunk(x, norm1_w, w_in, conv_w, q_norm_w, k_norm_w, A_log, dt_bias, dn_norm_w,
              w_out, norm2_w, router_w, router_b, w1, b1, w2, b2):
    for l in range(DEPTH):
        x = encoder_layer(x, norm1_w[l], w_in[l], conv_w[l], q_norm_w[l], k_norm_w[l], A_log[l],
                          dt_bias[l], dn_norm_w[l], w_out[l], norm2_w[l], router_w[l], router_b[l],
                          w1[l], b1[l], w2[l], b2[l])
    return x


def setup_inputs(seed: int = 0) -> dict:
    key = jax.random.key(seed)
    ks = jax.random.split(key, 20)
    nrm = jax.random.normal
    x_prompt = nrm(ks[0], (BATCH, SEQ, D_MODEL), jnp.float32)
    x_sample = nrm(ks[1], (DEC_BATCH, DEC_SEQ, D_MODEL), jnp.float32)
    norm1_w = 1.0 + 0.02 * nrm(ks[2], (DEPTH, D_MODEL), jnp.float32)
    w_in = nrm(ks[3], (DEPTH, D_MODEL, PROJ_W), jnp.float32) * D_MODEL ** -0.5
    conv_w = nrm(ks[4], (DEPTH, CONV_W, 3 * DN_W), jnp.float32) * CONV_W ** -0.5
    q_norm_w = 1.0 + 0.02 * nrm(ks[5], (DEPTH, HEAD_DIM), jnp.float32)
    k_norm_w = 1.0 + 0.02 * nrm(ks[6], (DEPTH, HEAD_DIM), jnp.float32)
    A_log = jnp.log(jax.random.uniform(ks[7], (DEPTH, 2, N_DN_HEADS), jnp.float32, 1.0, 16.0))
    dt = jnp.exp(jax.random.uniform(ks[8], (DEPTH, 2, N_DN_HEADS), jnp.float32,
                                    math.log(1e-3), math.log(1e-1)))
    dt_bias = dt + jnp.log(-jnp.expm1(-dt))
    dn_norm_w = 1.0 + 0.02 * nrm(ks[9], (DEPTH, DN_HEAD_DIM), jnp.float32)
    w_out = nrm(ks[10], (DEPTH, MIX_W, D_MODEL), jnp.float32) * MIX_W ** -0.5
    norm2_w = 1.0 + 0.02 * nrm(ks[11], (DEPTH, D_MODEL), jnp.float32)
    router_w = nrm(ks[12], (DEPTH, D_MODEL, N_EXPERTS), jnp.float32) * D_MODEL ** -0.5
    router_b = 0.01 * nrm(ks[13], (DEPTH, N_EXPERTS), jnp.float32)
    w1 = nrm(ks[14], (DEPTH, N_EXPERTS, D_MODEL, 2 * D_FF), jnp.float32) * D_MODEL ** -0.5
    b1 = 0.02 * nrm(ks[15], (DEPTH, N_EXPERTS, 2 * D_FF), jnp.float32)
    w2 = nrm(ks[16], (DEPTH, N_EXPERTS, D_FF, D_MODEL), jnp.float32) * D_FF ** -0.5
    b2 = 0.02 * nrm(ks[17], (DEPTH, N_EXPERTS, D_MODEL), jnp.float32)
    return {"x_prompt": x_prompt, "x_sample": x_sample, "norm1_w": norm1_w, "w_in": w_in,
            "conv_w": conv_w, "q_norm_w": q_norm_w, "k_norm_w": k_norm_w, "A_log": A_log,
            "dt_bias": dt_bias, "dn_norm_w": dn_norm_w, "w_out": w_out, "norm2_w": norm2_w,
            "router_w": router_w, "router_b": router_b, "w1": w1, "b1": b1, "w2": w2, "b2": b2}


def reference(x_prompt, x_sample, norm1_w, w_in, conv_w, q_norm_w, k_norm_w, A_log, dt_bias,
              dn_norm_w, w_out, norm2_w, router_w, router_b, w1, b1, w2, b2):
    y_prompt = run_trunk(x_prompt, norm1_w, w_in, conv_w, q_norm_w, k_norm_w, A_log, dt_bias,
                         dn_norm_w, w_out, norm2_w, router_w, router_b, w1, b1, w2, b2)
    y_sample = run_trunk(x_sample, norm1_w, w_in, conv_w, q_norm_w, k_norm_w, A_log, dt_bias,
                         dn_norm_w, w_out, norm2_w, router_w, router_b, w1, b1, w2, b2)
    return (y_prompt, y_sample)
```

```python
import functools
import math

import numpy as np
import jax
import jax.numpy as jnp
from jax import lax
from jax.experimental import pallas as pl
from jax.experimental.pallas import tpu as pltpu

F32 = jnp.float32
BF16 = jnp.bfloat16

D_MODEL = 1024
ATTN_W = 512
HEAD_DIM = 64
N_Q_HEADS = 8
N_KV_HEADS = 2
KV_W = 128
DN_W = 512
DN_HEAD_DIM = 128
N_DN_HEADS = 4
CONV_W = 5
CHUNK = 64
GRID_W = 64
ROPE_THETA = 10000.0
N_EXPERTS = 32
TOP_K = 4
D_FF = 1024
SWIGLU_LIMIT = 7.0
SWIGLU_ALPHA = 1.702
NORM_EPS = 1e-6

LANES = 128
PROJ_PAD = 2944
VMEM_LIMIT = 56 * 1024 * 1024

_C_Q = 0
_C_KV = ATTN_W
_C_DN = _C_KV + 2 * KV_W
_C_Z = _C_DN + 3 * DN_W
_C_BA = _C_Z + DN_W


def _cparams(sem):
    return pltpu.CompilerParams(dimension_semantics=sem, vmem_limit_bytes=VMEM_LIMIT)


def _rope_tables(T):
    t = np.arange(T)
    row = (t // GRID_W).astype(np.float64)
    col = (t % GRID_W).astype(np.float64)
    half = HEAD_DIM // 2
    freqs = ROPE_THETA ** (-np.arange(0, half, 2, dtype=np.float64) / half)
    lane = np.arange(LANES)
    d = lane % HEAD_DIM
    use_col = (d // half) == 1
    f = d % (half // 2)
    first = (d % half) < (half // 2)
    pos = np.where(use_col[None, :], col[:, None], row[:, None])
    ang = pos * freqs[f][None, :]
    cos = np.cos(ang)
    sin = np.sin(ang)
    sin_a = np.where(first[None, :], -sin, 0.0)
    sin_b = np.where(first[None, :], 0.0, sin)
    return (jnp.asarray(cos, F32), jnp.asarray(sin_a, F32), jnp.asarray(sin_b, F32))


def _head_mean_matrix():
    lane = np.arange(LANES)
    m = (lane[:, None] // HEAD_DIM == lane[None, :] // HEAD_DIM).astype(np.float32) / HEAD_DIM
    return jnp.asarray(m, BF16)


def _split_bf16(x):
    hi = x.astype(BF16)
    lo = (x - hi.astype(F32)).astype(BF16)
    return hi, lo


def _inproj_kernel(x_ref, n1w_ref, w_ref, qnw_ref, knw_ref, cos_ref, sa_ref, sb_ref, hm_ref,
                   q_ref, k_ref, v_ref, dn_ref, z_ref, ba_ref):
    x = x_ref[...]
    ms = jnp.mean(x * x, axis=-1, keepdims=True)
    hn = (x * lax.rsqrt(ms + NORM_EPS) * n1w_ref[...]).astype(BF16)

    def proj(a, b):
        return jnp.dot(hn, w_ref[:, a:b], preferred_element_type=F32)

    hm = hm_ref[...]
    cos = cos_ref[...]
    sa = sa_ref[...]
    sb = sb_ref[...]
    lane = lax.broadcasted_iota(jnp.int32, (1, LANES), 1)
    lo_half = lane < HEAD_DIM

    def norm_rope(xs, w):
        hi, lo = _split_bf16(xs * xs)
        msq = (jnp.dot(hi, hm, preferred_element_type=F32)
               + jnp.dot(lo, hm, preferred_element_type=F32))
        xn = xs * lax.rsqrt(msq + NORM_EPS) * w
        return (xn * cos + pltpu.roll(xn, LANES - 16, 1) * sa + pltpu.roll(xn, 16, 1) * sb)

    q = proj(_C_Q, _C_Q + ATTN_W)
    scale = HEAD_DIM ** -0.5
    for j in range(ATTN_W // LANES):
        qr = norm_rope(q[:, j * LANES:(j + 1) * LANES], qnw_ref[...] * scale)
        qs = pltpu.roll(qr, HEAD_DIM, 1)
        g = (2 * j) // (N_Q_HEADS // N_KV_HEADS)
        keep = lo_half if g == 0 else jnp.logical_not(lo_half)
        h0 = qr if g == 0 else qs
        h1 = qs if g == 0 else qr
        q_ref[:, (2 * j) * LANES:(2 * j + 1) * LANES] = jnp.where(keep, h0, 0.0).astype(BF16)
        q_ref[:, (2 * j + 1) * LANES:(2 * j + 2) * LANES] = jnp.where(keep, h1, 0.0).astype(BF16)

    kv = proj(_C_KV, _C_KV + 2 * KV_W)
    k_ref[...] = norm_rope(kv[:, :KV_W], knw_ref[...]).astype(BF16)
    v_ref[...] = kv[:, KV_W:].astype(BF16)
    dn_ref[...] = proj(_C_DN, _C_DN + 3 * DN_W).astype(BF16)
    z_ref[...] = proj(_C_Z, _C_Z + DN_W).astype(BF16)
    ba_ref[...] = proj(_C_BA, _C_BA + LANES)[:, :4 * N_DN_HEADS]


def _inproj(x2d, T, norm1_w, w_in_b, q_norm_w, k_norm_w, tm=512):
    n = x2d.shape[0]
    cos, sa, sb = _rope_tables(T)
    tm = min(tm, T)
    tpb = T // tm
    tab_spec = pl.BlockSpec((tm, LANES), lambda i: (i % tpb, 0))
    const = lambda shape: pl.BlockSpec(shape, lambda i: (0,) * len(shape))
    row = lambda w: pl.BlockSpec((tm, w), lambda i: (i, 0))
    qnw = jnp.tile(q_norm_w, LANES // HEAD_DIM)[None, :]
    knw = jnp.tile(k_norm_w, LANES // HEAD_DIM)[None, :]
    return pl.pallas_call(
        _inproj_kernel,
        grid=(n // tm,),
        in_specs=[row(D_MODEL), const((1, D_MODEL)), const((D_MODEL, PROJ_PAD)),
                  const((1, LANES)), const((1, LANES)), tab_spec, tab_spec, tab_spec,
                  const((LANES, LANES))],
        out_specs=[row(N_Q_HEADS * LANES), row(KV_W), row(KV_W), row(3 * DN_W), row(DN_W),
                   row(4 * N_DN_HEADS)],
        out_shape=[jax.ShapeDtypeStruct((n, N_Q_HEADS * LANES), BF16),
                   jax.ShapeDtypeStruct((n, KV_W), BF16),
                   jax.ShapeDtypeStruct((n, KV_W), BF16),
                   jax.ShapeDtypeStruct((n, 3 * DN_W), BF16),
                   jax.ShapeDtypeStruct((n, DN_W), BF16),
                   jax.ShapeDtypeStruct((n, 4 * N_DN_HEADS), F32)],
        compiler_params=_cparams(("parallel",)),
        name="inproj",
    )(x2d, norm1_w[None, :], w_in_b, qnw, knw, cos, sa, sb, _head_mean_matrix())


def _attn_kernel(q_ref, k_ref, v_ref, o_ref, m_sc, acc_sc, *, tq, tk, n_kc):
    q = jnp.concatenate([q_ref[0, :, h * LANES:(h + 1) * LANES] for h in range(N_Q_HEADS)], axis=0)
    m_sc[...] = jnp.full(m_sc.shape, -jnp.inf, F32)
    acc_sc[...] = jnp.zeros(acc_sc.shape, F32)
    ones = jnp.ones((tk, LANES), BF16)

    def body(c, carry):
        start = pl.multiple_of(c * tk, tk)
        kc = k_ref[0, pl.ds(start, tk), :]
        vc = v_ref[0, pl.ds(start, tk), :]
        s = lax.dot_general(q, kc, (((1,), (1,)), ((), ())), preferred_element_type=F32)
        m_prev = m_sc[...]
        m_next = jnp.maximum(m_prev, jnp.max(s, axis=1, keepdims=True))
        alpha = jnp.exp(m_prev - m_next)
        p = jnp.concatenate(
            [jnp.exp(s[:, j * LANES:(j + 1) * LANES] - m_next) for j in range(tk // LANES)], axis=1)
        v_aug = jnp.concatenate([vc, ones], axis=1)
        pv = jnp.dot(p.astype(BF16), v_aug, preferred_element_type=F32)
        acc_sc[...] = acc_sc[...] * jnp.concatenate([alpha, alpha], axis=1) + pv
        m_sc[...] = m_next
        return carry

    lax.fori_loop(0, n_kc, body, 0)

    lane = lax.broadcasted_iota(jnp.int32, (1, LANES), 1)
    lo_half = lane < HEAD_DIM
    outs = []
    for h in range(N_Q_HEADS):
        a = acc_sc[h * tq:(h + 1) * tq, :]
        outs.append(a[:, :LANES] / a[:, LANES:])
    for j in range(N_Q_HEADS // 2):
        g = (2 * j) // (N_Q_HEADS // N_KV_HEADS)
        a, b = outs[2 * j], outs[2 * j + 1]
        if g == 0:
            b = pltpu.roll(b, HEAD_DIM, 1)
        else:
            a = pltpu.roll(a, HEAD_DIM, 1)
        o_ref[0, :, j * LANES:(j + 1) * LANES] = jnp.where(lo_half, a, b).astype(BF16)


def _attention(q, k, v, tq=128, tk=512):
    B, T, _ = q.shape
    tk = min(tk, T)
    kern = functools.partial(_attn_kernel, tq=tq, tk=tk, n_kc=T // tk)
    return pl.pallas_call(
        kern,
        grid=(B, T // tq),
        in_specs=[pl.BlockSpec((1, tq, N_Q_HEADS * LANES), lambda b, i: (b, i, 0)),
                  pl.BlockSpec((1, T, KV_W), lambda b, i: (b, 0, 0)),
                  pl.BlockSpec((1, T, KV_W), lambda b, i: (b, 0, 0))],
        out_specs=pl.BlockSpec((1, tq, ATTN_W), lambda b, i: (b, i, 0)),
        out_shape=jax.ShapeDtypeStruct((B, T, ATTN_W), BF16),
        scratch_shapes=[pltpu.VMEM((N_Q_HEADS * tq, LANES), F32),
                        pltpu.VMEM((N_Q_HEADS * tq, 2 * LANES), F32)],
        compiler_params=_cparams(("parallel", "parallel")),
        name="attention",
    )(q, k, v)


_GROUP = 8


def _dn_masks():
    r = lax.broadcasted_iota(jnp.int32, (CHUNK, CHUNK), 0)
    c = lax.broadcasted_iota(jnp.int32, (CHUNK, CHUNK), 1)
    return r, c


def _dn_kernel(xq_ref, xk_ref, xv_ref, z_ref, ba_ref, cw_ref, ad_ref, nw_ref, o_ref,
               xpad, qn_sc, kn_sc, vn_sc, u_sc, wq_sc, kd_sc, qk_sc, gt_sc, s_sc, of_sc, ob_sc,
               *, T):
    n_chunks = T // CHUNK
    n_groups = n_chunks // _GROUP
    gsz = _GROUP * CHUNK
    h = pl.program_id(1)

    def conv_pass(x_ref, w_idx, dst, mode):
        xpad[0:8, :] = jnp.zeros((8, LANES), F32)
        xpad[T + 8:T + 16, :] = jnp.zeros((8, LANES), F32)
        blk = 256

        def cp(i, c):
            s = pl.multiple_of(i * blk, blk)
            xpad[pl.ds(s + 8, blk), :] = x_ref[0, pl.ds(s, blk), :].astype(F32)
            return c
        lax.fori_loop(0, T // blk, cp, 0)
        w = cw_ref[w_idx]
        L = blk + 16

        def cv(i, c):
            s = pl.multiple_of(i * blk, blk)
            win = xpad[pl.ds(s, L), :]
            acc = None
            for j in range(CONV_W):
                sh = (2 - j) % L
                y = win if sh == 0 else pltpu.roll(win, sh, 0)
                term = y[8:8 + blk, :] * w[j:j + 1, :]
                acc = term if acc is None else acc + term
            y = acc / (1.0 + jnp.exp(-acc))
            if mode == "q":
                y = y * lax.rsqrt(jnp.sum(y * y, axis=-1, keepdims=True) + NORM_EPS) * (DN_HEAD_DIM ** -0.5)
            elif mode == "k":
                y = y * lax.rsqrt(jnp.sum(y * y, axis=-1, keepdims=True) + NORM_EPS)
            dst[pl.ds(s, blk), :] = y
            return c
        lax.fori_loop(0, T // blk, cv, 0)

    conv_pass(xq_ref, 0, qn_sc, "q")
    conv_pass(xk_ref, 1, kn_sc, "k")
    conv_pass(xv_ref, 2, vn_sc, "v")

    r, c = _dn_masks()
    incl = [(c <= r), (c >= r)]
    strict = [(c < r), (c > r)]
    eye = (r == c).astype(F32)
    cum_ops = []
    for d in range(2):
        rem = (c > r) if d == 0 else (c < r)
        cum_ops.append(jnp.concatenate([incl[d].astype(BF16), rem.astype(BF16)], axis=0))
    r2 = lax.broadcasted_iota(jnp.int32, (CHUNK, LANES), 0)
    c2 = lax.broadcasted_iota(jnp.int32, (CHUNK, LANES), 1)
    gs_mask = [(r2 > c2), jnp.logical_and(r2 < c2, c2 < CHUNK)]

    def prep_chunk(d, tok0, slot):
        q = qn_sc[pl.ds(tok0, CHUNK), :]
        k = kn_sc[pl.ds(tok0, CHUNK), :]
        v = vn_sc[pl.ds(tok0, CHUNK), :]
        ba = ba_ref[0, 0, pl.ds(tok0, CHUNK), :]
        beta = 1.0 / (1.0 + jnp.exp(-ba[:, d:d + 1]))
        a_in = ba[:, 2 + d:3 + d] + ad_ref[0, 2 + d:3 + d, :][:, 0:1]
        sp = jnp.maximum(a_in, 0.0) + jnp.log(1.0 + jnp.exp(-jnp.abs(a_in)))
        g = -jnp.exp(ad_ref[0, d:d + 1, :][:, 0:1]) * sp
        gb = jnp.broadcast_to(g, (CHUNK, LANES))
        gs = jnp.where(gs_mask[d], gb, 0.0)
        gb_hi, gb_lo = _split_bf16(gb)
        gs_hi, gs_lo = _split_bf16(gs)
        rhs = jnp.concatenate([gb_hi, gb_lo, gs_hi, gs_lo], axis=1)
        cs = jnp.dot(cum_ops[d], rhs, preferred_element_type=F32)
        gc = cs[:CHUNK, 0:LANES] + cs[:CHUNK, LANES:2 * LANES]
        grem = cs[CHUNK:, 0:LANES] + cs[CHUNK:, LANES:2 * LANES]
        dmat = cs[:CHUNK, 2 * LANES:2 * LANES + CHUNK] + cs[:CHUNK, 3 * LANES:3 * LANES + CHUNK]
        decay = jnp.exp(dmat)
        egc = jnp.exp(gc)
        kb = k * beta
        vb = v * beta
        kq = lax.dot_general(jnp.concatenate([kb, q], axis=0).astype(BF16), k.astype(BF16),
                             (((1,), (1,)), ((), ())), preferred_element_type=F32)
        a = jnp.where(strict[d], kq[:CHUNK] * decay, 0.0)
        qk = jnp.where(incl[d], kq[CHUNK:] * decay, 0.0)
        x = eye - a
        p = a
        for _ in range(5):
            pb = p.astype(BF16)
            p = jnp.dot(pb, pb, preferred_element_type=F32)
            x = x + jnp.dot(x.astype(BF16), p.astype(BF16), preferred_element_type=F32)
        uw = jnp.dot(x.astype(BF16), jnp.concatenate([vb, kb * egc], axis=1).astype(BF16),
                     preferred_element_type=F32)
        row0 = slot * CHUNK
        u_sc[d, pl.ds(row0, CHUNK), :] = uw[:, :LANES]
        wq_sc[d, pl.ds(2 * row0, CHUNK), :] = uw[:, LANES:].astype(BF16)
        wq_sc[d, pl.ds(2 * row0 + CHUNK, CHUNK), :] = (q * egc).astype(BF16)
        kd_sc[d, pl.ds(row0, CHUNK), :] = (k * jnp.exp(grem)).astype(BF16)
        qk_sc[d, pl.ds(row0, CHUNK), :] = qk.astype(BF16)
        tot_row = gc[CHUNK - 1:CHUNK, :] if d == 0 else gc[0:1, :]
        gt_sc[d, pl.ds(slot * 8, 8), :] = jnp.broadcast_to(jnp.exp(tot_row), (8, LANES))

    s_sc[...] = jnp.zeros(s_sc.shape, F32)

    def group_body(gi, carry):
        base_f = pl.multiple_of(gi * gsz, gsz)
        base_b = pl.multiple_of((n_groups - 1 - gi) * gsz, gsz)
        for j in range(_GROUP):
            prep_chunk(0, pl.multiple_of(base_f + j * CHUNK, CHUNK), j)
            prep_chunk(1, pl.multiple_of(base_b + (_GROUP - 1 - j) * CHUNK, CHUNK), j)
        s_f = s_sc[0]
        s_b = s_sc[1]
        for j in range(_GROUP):
            new = []
            for d, s_mat in ((0, s_f), (1, s_b)):
                sw = jnp.dot(wq_sc[d, pl.ds(2 * j * CHUNK, 2 * CHUNK), :], s_mat.astype(BF16),
                             preferred_element_type=F32)
                v_new = u_sc[d, pl.ds(j * CHUNK, CHUNK), :] - sw[:CHUNK]
                vnb = v_new.astype(BF16)
                o = sw[CHUNK:] + jnp.dot(qk_sc[d, pl.ds(j * CHUNK, CHUNK), :], vnb,
                                         preferred_element_type=F32)
                kd = kd_sc[d, pl.ds(j * CHUNK, CHUNK), :]
                s_mat = (s_mat * gt_sc[d, pl.ds(j * 8, 1), :]
                         + lax.dot_general(kd, vnb, (((0,), (0,)), ((), ())),
                                           preferred_element_type=F32))
                new.append(s_mat)
                if d == 0:
                    of_sc[pl.ds(pl.multiple_of(base_f + j * CHUNK, CHUNK), CHUNK), :] = o
                else:
                    ob_sc[pl.ds(pl.multiple_of(base_b + (_GROUP - 1 - j) * CHUNK, CHUNK), CHUNK), :] = o
            s_f, s_b = new
        s_sc[0] = s_f
        s_sc[1] = s_b
        return carry

    lax.fori_loop(0, n_groups, group_body, 0)

    blk = 256

    def fin(i, carry):
        s = pl.multiple_of(i * blk, blk)
        o = of_sc[pl.ds(s, blk), :] + ob_sc[pl.ds(s, blk), :]
        o = o * lax.rsqrt(jnp.mean(o * o, axis=-1, keepdims=True) + NORM_EPS) * nw_ref[...]
        z = z_ref[0, pl.ds(s, blk), :].astype(F32)
        o_ref[0, pl.ds(s, blk), :] = (o * (z / (1.0 + jnp.exp(-z)))).astype(BF16)
        return carry
    lax.fori_loop(0, T // blk, fin, 0)


def _deltanet(dn, z, ba, conv_w, A_log, dt_bias, dn_norm_w):
    B, T, _ = dn.shape
    H = N_DN_HEADS
    ba_t = ba.reshape(B, T, 2, 2, H).transpose(0, 4, 1, 2, 3).reshape(B, H, T, 4)
    cw = jnp.pad(conv_w, ((0, 8 - CONV_W), (0, 0))).reshape(8, 3, H, DN_HEAD_DIM).transpose(2, 1, 0, 3)
    cw = cw.reshape(H * 3, 8, DN_HEAD_DIM)
    ad = jnp.concatenate([A_log, dt_bias], axis=0).T
    ad = jnp.broadcast_to(ad[:, :, None], (H, 4, LANES))
    ad = jnp.pad(ad, ((0, 0), (0, 4), (0, 0)))
    gsz = _GROUP * CHUNK
    kern = functools.partial(_dn_kernel, T=T)
    col = lambda off: pl.BlockSpec((1, T, DN_HEAD_DIM), lambda b, h, off=off: (b, 0, off + h))
    return pl.pallas_call(
        kern,
        grid=(B, H),
        in_specs=[col(0), col(H), col(2 * H),
                  pl.BlockSpec((1, T, DN_HEAD_DIM), lambda b, h: (b, 0, h)),
                  pl.BlockSpec((1, 1, T, 4), lambda b, h: (b, h, 0, 0)),
                  pl.BlockSpec((3, 8, DN_HEAD_DIM), lambda b, h: (h, 0, 0)),
                  pl.BlockSpec((1, 8, LANES), lambda b, h: (h, 0, 0)),
                  pl.BlockSpec((1, DN_HEAD_DIM), lambda b, h: (0, 0))],
        out_specs=pl.BlockSpec((1, T, DN_HEAD_DIM), lambda b, h: (b, 0, h)),
        out_shape=jax.ShapeDtypeStruct((B, T, DN_W), BF16),
        scratch_shapes=[pltpu.VMEM((T + 16, LANES), F32),
                        pltpu.VMEM((T, LANES), F32),
                        pltpu.VMEM((T, LANES), F32),
                        pltpu.VMEM((T, LANES), F32),
                        pltpu.VMEM((2, gsz, LANES), F32),
                        pltpu.VMEM((2, 2 * gsz, LANES), BF16),
                        pltpu.VMEM((2, gsz, LANES), BF16),
                        pltpu.VMEM((2, gsz, CHUNK), BF16),
                        pltpu.VMEM((2, _GROUP * 8, LANES), F32),
                        pltpu.VMEM((2, DN_HEAD_DIM, DN_HEAD_DIM), F32),
                        pltpu.VMEM((T, LANES), F32),
                        pltpu.VMEM((T, LANES), F32)],
        compiler_params=_cparams(("parallel", "parallel")),
        name="deltanet",
    )(dn, dn, dn, z, ba_t, cw, ad, dn_norm_w[None, :])


def _outproj_kernel(x_ref, at_ref, dn_ref, wo_ref, n2w_ref, rw_ref, rb_ref,
                    h_ref, h2_ref, idx_ref, gate_ref):
    mix = (jnp.dot(at_ref[...], wo_ref[0:ATTN_W, :], preferred_element_type=F32)
           + jnp.dot(dn_ref[...], wo_ref[ATTN_W:, :], preferred_element_type=F32))
    hres = x_ref[...] + mix
    h_ref[...] = hres
    h2 = hres * lax.rsqrt(jnp.mean(hres * hres, axis=-1, keepdims=True) + NORM_EPS) * n2w_ref[...]
    h2_ref[...] = h2
    logits = jnp.dot(h2, rw_ref[...], preferred_element_type=F32,
                     precision=lax.Precision.HIGHEST) + rb_ref[...]
    lane = lax.broadcasted_iota(jnp.int32, logits.shape, 1).astype(F32)
    vals = []
    idxs = []
    l = logits
    for _ in range(TOP_K):
        m = jnp.max(l, axis=-1, keepdims=True)
        i = jnp.min(jnp.where(l == m, lane, float(LANES)), axis=-1, keepdims=True)
        vals.append(m)
        idxs.append(i)
        l = jnp.where(lane == i, -jnp.inf, l)
    es = [jnp.exp(v - vals[0]) for v in vals]
    den = es[0] + es[1] + es[2] + es[3]
    idx_ref[...] = jnp.concatenate(idxs, axis=1).astype(jnp.int32)
    gate_ref[...] = jnp.concatenate([e / den for e in es], axis=1)


def _outproj_router(x2d, attn2d, dn2d, w_out_b, norm2_w, router_w, router_b, tm=512):
    n = x2d.shape[0]
    rw = jnp.pad(router_w, ((0, 0), (0, LANES - N_EXPERTS)))
    rb = jnp.pad(router_b, (0, LANES - N_EXPERTS), constant_values=-jnp.inf)[None, :]
    const = lambda shape: pl.BlockSpec(shape, lambda i: (0,) * len(shape))
    row = lambda w: pl.BlockSpec((tm, w), lambda i: (i, 0))
    return pl.pallas_call(
        _outproj_kernel,
        grid=(n // tm,),
        in_specs=[row(D_MODEL), row(ATTN_W), row(DN_W), const((D_MODEL, D_MODEL)),
                  const((1, D_MODEL)), const((D_MODEL, LANES)), const((1, LANES))],
        out_specs=[row(D_MODEL), row(D_MODEL), row(TOP_K), row(TOP_K)],
        out_shape=[jax.ShapeDtypeStruct((n, D_MODEL), F32),
                   jax.ShapeDtypeStruct((n, D_MODEL), F32),
                   jax.ShapeDtypeStruct((n, TOP_K), jnp.int32),
                   jax.ShapeDtypeStruct((n, TOP_K), F32)],
        compiler_params=_cparams(("parallel",)),
        name="outproj_router",
    )(x2d, attn2d, dn2d, w_out_b, norm2_w[None, :], rw, rb)


_BM = 256


def _moe_kernel(be_ref, nu_ref, cnt_ref, tok_ref, tokn_ref, dst_ref, h2_hbm, w1_ref, b1_ref, w2_ref,
                b2_ref, yk_hbm, xbuf, ybuf, gsem, ssem):
    i = pl.program_id(0)
    n_used = nu_ref[0]
    slot = i % 2

    def gather(idx_ref, sl):
        def body(rr, c):
            t = idx_ref[0, 0, rr]
            pltpu.make_async_copy(h2_hbm.at[pl.ds(t, 1)], xbuf.at[sl, pl.ds(rr, 1)], gsem.at[sl]).start()
            return c
        lax.fori_loop(0, _BM, body, 0, unroll=8)

    def wait_gather(sl):
        pltpu.make_async_copy(h2_hbm.at[pl.ds(0, _BM)], xbuf.at[sl], gsem.at[sl]).wait()

    def row_scatter(sl, rr):
        dd = dst_ref[0, 0, rr]
        return pltpu.make_async_copy(ybuf.at[sl, pl.ds(rr, 1)], yk_hbm.at[pl.ds(dd, 1)], ssem.at[sl])

    def start_scatter(sl, n_valid):
        def body(rr, c):
            row_scatter(sl, rr).start()
            return c

        @pl.when(n_valid == _BM)
        def _():
            lax.fori_loop(0, _BM, body, 0, unroll=8)

        @pl.when(n_valid < _BM)
        def _():
            lax.fori_loop(0, n_valid, body, 0)

    def wait_scatter(sl, n_valid):
        @pl.when(n_valid == _BM)
        def _():
            pltpu.make_async_copy(ybuf.at[sl], yk_hbm.at[pl.ds(0, _BM)], ssem.at[sl]).wait()

        @pl.when(n_valid < _BM)
        def _():
            def body(rr, c):
                pltpu.make_async_copy(ybuf.at[sl, pl.ds(0, 1)], yk_hbm.at[pl.ds(0, 1)], ssem.at[sl]).wait()
                return c
            lax.fori_loop(0, n_valid, body, 0)

    @pl.when(i == 0)
    def _():
        gather(tok_ref, 0)

    @pl.when(i + 1 < n_used)
    def _():
        gather(tokn_ref, 1 - slot)

    @pl.when(i < n_used)
    def _():
        wait_gather(slot)

        @pl.when(i >= 2)
        def _():
            wait_scatter(slot, cnt_ref[jnp.maximum(i - 2, 0)])

        x = xbuf[slot].astype(BF16)
        hh = jnp.dot(x, w1_ref[0], preferred_element_type=F32) + b1_ref[0]
        gate = jnp.minimum(hh[:, :D_FF], SWIGLU_LIMIT)
        up = jnp.clip(hh[:, D_FF:], -SWIGLU_LIMIT, SWIGLU_LIMIT)
        act = gate * (1.0 / (1.0 + jnp.exp(-SWIGLU_ALPHA * gate))) * (up + 1.0)
        ybuf[slot] = jnp.dot(act.astype(BF16), w2_ref[0], preferred_element_type=F32) + b2_ref[0]
        start_scatter(slot, cnt_ref[i])

        @pl.when(i == n_used - 1)
        def _():
            wait_scatter(slot, cnt_ref[i])

            @pl.when(i >= 1)
            def _():
                wait_scatter(1 - slot, cnt_ref[jnp.maximum(i - 1, 0)])


def _moe_blocks(n_tok):
    n_assign = n_tok * TOP_K
    return n_assign // _BM + N_EXPERTS


def _routing_plan(idx):
    n_tok = idx.shape[0]
    n_assign = n_tok * TOP_K
    nb = _moe_blocks(n_tok)
    n_rows = nb * _BM
    flat_e = idx.reshape(-1)
    order = jnp.argsort(flat_e, stable=True).astype(jnp.int32)
    sorted_e = flat_e[order]
    counts = jnp.bincount(flat_e, length=N_EXPERTS).astype(jnp.int32)
    padded = (counts + _BM - 1) // _BM * _BM
    start = jnp.cumsum(counts) - counts
    pad_end = jnp.cumsum(padded)
    pad_start = pad_end - padded
    dest = pad_start[sorted_e] + jnp.arange(n_assign, dtype=jnp.int32) - start[sorted_e]
    row_tok = jnp.zeros((n_rows,), jnp.int32).at[dest].set(order // TOP_K)
    row_dst = jnp.zeros((n_rows,), jnp.int32).at[dest].set(order)
    n_used = (pad_end[-1] // _BM).astype(jnp.int32)
    blk = jnp.minimum(jnp.arange(nb, dtype=jnp.int32), n_used - 1) * _BM
    block_e = jnp.minimum(jnp.searchsorted(pad_end, blk, side="right"), N_EXPERTS - 1).astype(jnp.int32)
    cnt = jnp.clip(pad_start[block_e] + counts[block_e] - blk, 0, _BM).astype(jnp.int32)
    return (block_e, n_used.reshape(1), cnt, row_tok.reshape(nb, 1, _BM), row_dst.reshape(nb, 1, _BM))


def _moe(h2, idx, w1_b, b1, w2_b, b2):
    n_tok = h2.shape[0]
    n_assign = n_tok * TOP_K
    nb = _moe_blocks(n_tok)
    block_e, n_used, cnt, row_tok, row_dst = _routing_plan(idx)
    smem_blk = lambda f: pl.BlockSpec((1, 1, _BM), f, memory_space=pltpu.SMEM)
    grid_spec = pltpu.PrefetchScalarGridSpec(
        num_scalar_prefetch=3,
        grid=(nb,),
        in_specs=[smem_blk(lambda i, be, nu, ct: (i, 0, 0)),
                  smem_blk(lambda i, be, nu, ct: (jnp.minimum(i + 1, nb - 1), 0, 0)),
                  smem_blk(lambda i, be, nu, ct: (i, 0, 0)),
                  pl.BlockSpec(memory_space=pl.ANY),
                  pl.BlockSpec((1, D_MODEL, 2 * D_FF), lambda i, be, nu, ct: (be[i], 0, 0)),
                  pl.BlockSpec((1, 1, 2 * D_FF), lambda i, be, nu, ct: (be[i], 0, 0)),
                  pl.BlockSpec((1, D_FF, D_MODEL), lambda i, be, nu, ct: (be[i], 0, 0)),
                  pl.BlockSpec((1, 1, D_MODEL), lambda i, be, nu, ct: (be[i], 0, 0))],
        out_specs=pl.BlockSpec(memory_space=pl.ANY),
        scratch_shapes=[pltpu.VMEM((2, _BM, D_MODEL), F32),
                        pltpu.VMEM((2, _BM, D_MODEL), F32),
                        pltpu.SemaphoreType.DMA((2,)),
                        pltpu.SemaphoreType.DMA((2,))],
    )
    return pl.pallas_call(
        _moe_kernel,
        grid_spec=grid_spec,
        out_shape=jax.ShapeDtypeStruct((n_assign, D_MODEL), F32),
        compiler_params=_cparams(("arbitrary",)),
        name="moe_experts",
    )(block_e, n_used, cnt, row_tok, row_tok, row_dst, h2, w1_b, b1[:, None, :], w2_b, b2[:, None, :])


def _combine_kernel(h_ref, yk_ref, g_ref, o_ref):
    g = g_ref[...]
    acc = h_ref[...]
    for k in range(TOP_K):
        acc = acc + g[:, k:k + 1] * yk_ref[:, k * D_MODEL:(k + 1) * D_MODEL]
    o_ref[...] = acc


def _combine(h, yk, gates, tm=256):
    n = h.shape[0]
    ykv = yk.reshape(n, TOP_K * D_MODEL)
    row = lambda w: pl.BlockSpec((tm, w), lambda i: (i, 0))
    return pl.pallas_call(
        _combine_kernel,
        grid=(n // tm,),
        in_specs=[row(D_MODEL), row(TOP_K * D_MODEL), row(TOP_K)],
        out_specs=row(D_MODEL),
        out_shape=jax.ShapeDtypeStruct((n, D_MODEL), F32),
        compiler_params=_cparams(("parallel",)),
        name="moe_combine",
    )(h, ykv, gates)


def _layer(x, p):
    B, T, D = x.shape
    x2d = x.reshape(B * T, D)
    q, k, v, dn, z, ba = _inproj(x2d, T, p["norm1_w"], p["w_in"], p["q_norm_w"], p["k_norm_w"])
    attn = _attention(q.reshape(B, T, -1), k.reshape(B, T, -1), v.reshape(B, T, -1))
    dno = _deltanet(dn.reshape(B, T, -1), z.reshape(B, T, -1), ba.reshape(B, T, -1),
                    p["conv_w"], p["A_log"], p["dt_bias"], p["dn_norm_w"])
    h, h2, idx, gates = _outproj_router(x2d, attn.reshape(B * T, -1), dno.reshape(B * T, -1),
                                        p["w_out"], p["norm2_w"], p["router_w"], p["router_b"])
    yk = _moe(h2, idx, p["w1"], p["b1"], p["w2"], p["b2"])
    y = _combine(h, yk, gates)
    return y.reshape(B, T, D)


def kernel(x_prompt, x_sample, norm1_w, w_in, conv_w, q_norm_w, k_norm_w, A_log, dt_bias, dn_norm_w,
           w_out, norm2_w, router_w, router_b, w1, b1, w2, b2):
    l = 0
    p = {
        "norm1_w": norm1_w[l],
        "w_in": jnp.pad(w_in[l], ((0, 0), (0, PROJ_PAD - w_in.shape[-1]))).astype(BF16),
        "conv_w": conv_w[l], "q_norm_w": q_norm_w[l], "k_norm_w": k_norm_w[l],
        "A_log": A_log[l], "dt_bias": dt_bias[l], "dn_norm_w": dn_norm_w[l],
        "w_out": w_out[l].astype(BF16), "norm2_w": norm2_w[l],
        "router_w": router_w[l], "router_b": router_b[l],
        "w1": w1[l].astype(BF16), "b1": b1[l], "w2": w2[l].astype(BF16), "b2": b2[l],
    }
    return (_layer(x_prompt, p), _layer(x_sample, p))
```

```python
import functools
import math

import numpy as np
import jax
import jax.numpy as jnp
from jax import lax
from jax.experimental import pallas as pl
from jax.experimental.pallas import tpu as pltpu

F32 = jnp.float32
BF16 = jnp.bfloat16

D_MODEL = 1024
ATTN_W = 512
HEAD_DIM = 64
N_Q_HEADS = 8
N_KV_HEADS = 2
KV_W = 128
DN_W = 512
DN_HEAD_DIM = 128
N_DN_HEADS = 4
CONV_W = 5
CHUNK = 64
GRID_W = 64
ROPE_THETA = 10000.0
N_EXPERTS = 32
TOP_K = 4
D_FF = 1024
SWIGLU_LIMIT = 7.0
SWIGLU_ALPHA = 1.702
NORM_EPS = 1e-6

LANES = 128
PROJ_PAD = 2944
VMEM_LIMIT = 56 * 1024 * 1024

_C_Q = 0
_C_KV = ATTN_W
_C_DN = _C_KV + 2 * KV_W
_C_Z = _C_DN + 3 * DN_W
_C_BA = _C_Z + DN_W


def _cparams(sem):
    return pltpu.CompilerParams(dimension_semantics=sem, vmem_limit_bytes=VMEM_LIMIT)


def _rope_tables(T):
    t = np.arange(T)
    row = (t // GRID_W).astype(np.float64)
    col = (t % GRID_W).astype(np.float64)
    half = HEAD_DIM // 2
    freqs = ROPE_THETA ** (-np.arange(0, half, 2, dtype=np.float64) / half)
    lane = np.arange(LANES)
    d = lane % HEAD_DIM
    use_col = (d // half) == 1
    f = d % (half // 2)
    first = (d % half) < (half // 2)
    pos = np.where(use_col[None, :], col[:, None], row[:, None])
    ang = pos * freqs[f][None, :]
    cos = np.cos(ang)
    sin = np.sin(ang)
    sin_a = np.where(first[None, :], -sin, 0.0)
    sin_b = np.where(first[None, :], 0.0, sin)
    return (jnp.asarray(cos, F32), jnp.asarray(sin_a, F32), jnp.asarray(sin_b, F32))


def _head_mean_matrix():
    lane = np.arange(LANES)
    m = (lane[:, None] // HEAD_DIM == lane[None, :] // HEAD_DIM).astype(np.float32) / HEAD_DIM
    return jnp.asarray(m, BF16)


def _split_bf16(x):
    hi = x.astype(BF16)
    lo = (x - hi.astype(F32)).astype(BF16)
    return hi, lo


def _inproj_kernel(x_ref, n1w_ref, w_ref, qnw_ref, knw_ref, cos_ref, sa_ref, sb_ref, hm_ref,
                   q_ref, k_ref, v_ref, dn_ref, z_ref, ba_ref):
    x = x_ref[...]
    ms = jnp.mean(x * x, axis=-1, keepdims=True)
    hn = (x * lax.rsqrt(ms + NORM_EPS) * n1w_ref[...]).astype(BF16)

    def proj(a, b):
        return jnp.dot(hn, w_ref[:, a:b], preferred_element_type=F32)

    hm = hm_ref[...]
    cos = cos_ref[...]
    sa = sa_ref[...]
    sb = sb_ref[...]
    lane = lax.broadcasted_iota(jnp.int32, (1, LANES), 1)
    lo_half = lane < HEAD_DIM

    def norm_rope(xs, w):
        hi, lo = _split_bf16(xs * xs)
        msq = (jnp.dot(hi, hm, preferred_element_type=F32)
               + jnp.dot(lo, hm, preferred_element_type=F32))
        xn = xs * lax.rsqrt(msq + NORM_EPS) * w
        return (xn * cos + pltpu.roll(xn, LANES - 16, 1) * sa + pltpu.roll(xn, 16, 1) * sb)

    q = proj(_C_Q, _C_Q + ATTN_W)
    scale = HEAD_DIM ** -0.5
    for j in range(ATTN_W // LANES):
        qr = norm_rope(q[:, j * LANES:(j + 1) * LANES], qnw_ref[...] * scale)
        qs = pltpu.roll(qr, HEAD_DIM, 1)
        g = (2 * j) // (N_Q_HEADS // N_KV_HEADS)
        keep = lo_half if g == 0 else jnp.logical_not(lo_half)
        h0 = qr if g == 0 else qs
        h1 = qs if g == 0 else qr
        q_ref[:, (2 * j) * LANES:(2 * j + 1) * LANES] = jnp.where(keep, h0, 0.0).astype(BF16)
        q_ref[:, (2 * j + 1) * LANES:(2 * j + 2) * LANES] = jnp.where(keep, h1, 0.0).astype(BF16)

    kv = proj(_C_KV, _C_KV + 2 * KV_W)
    k_ref[...] = norm_rope(kv[:, :KV_W], knw_ref[...]).astype(BF16)
    v_ref[...] = kv[:, KV_W:].astype(BF16)
    dn_ref[...] = proj(_C_DN, _C_DN + 3 * DN_W).astype(BF16)
    z_ref[...] = proj(_C_Z, _C_Z + DN_W).astype(BF16)
    ba_ref[...] = proj(_C_BA, _C_BA + LANES)[:, :4 * N_DN_HEADS]


def _inproj(x2d, T, norm1_w, w_in_b, q_norm_w, k_norm_w, tm=512):
    n = x2d.shape[0]
    cos, sa, sb = _rope_tables(T)
    tm = min(tm, T)
    tpb = T // tm
    tab_spec = pl.BlockSpec((tm, LANES), lambda i: (i % tpb, 0))
    const = lambda shape: pl.BlockSpec(shape, lambda i: (0,) * len(shape))
    row = lambda w: pl.BlockSpec((tm, w), lambda i: (i, 0))
    qnw = jnp.tile(q_norm_w, LANES // HEAD_DIM)[None, :]
    knw = jnp.tile(k_norm_w, LANES // HEAD_DIM)[None, :]
    return pl.pallas_call(
        _inproj_kernel,
        grid=(n // tm,),
        in_specs=[row(D_MODEL), const((1, D_MODEL)), const((D_MODEL, PROJ_PAD)),
                  const((1, LANES)), const((1, LANES)), tab_spec, tab_spec, tab_spec,
                  const((LANES, LANES))],
        out_specs=[row(N_Q_HEADS * LANES), row(KV_W), row(KV_W), row(3 * DN_W), row(DN_W),
                   row(4 * N_DN_HEADS)],
        out_shape=[jax.ShapeDtypeStruct((n, N_Q_HEADS * LANES), BF16),
                   jax.ShapeDtypeStruct((n, KV_W), BF16),
                   jax.ShapeDtypeStruct((n, KV_W), BF16),
                   jax.ShapeDtypeStruct((n, 3 * DN_W), BF16),
                   jax.ShapeDtypeStruct((n, DN_W), BF16),
                   jax.ShapeDtypeStruct((n, 4 * N_DN_HEADS), F32)],
        compiler_params=_cparams(("parallel",)),
        name="inproj",
    )(x2d, norm1_w[None, :], w_in_b, qnw, knw, cos, sa, sb, _head_mean_matrix())


def _attn_kernel(q_ref, k_ref, v_ref, o_ref, m_sc, acc_sc, *, tq, tk, n_kc):
    q = jnp.concatenate([q_ref[0, :, h * LANES:(h + 1) * LANES] for h in range(N_Q_HEADS)], axis=0)
    m_sc[...] = jnp.full(m_sc.shape, -jnp.inf, F32)
    acc_sc[...] = jnp.zeros(acc_sc.shape, F32)
    ones = jnp.ones((tk, LANES), BF16)

    def body(c, carry):
        start = pl.multiple_of(c * tk, tk)
        kc = k_ref[0, pl.ds(start, tk), :]
        vc = v_ref[0, pl.ds(start, tk), :]
        s = lax.dot_general(q, kc, (((1,), (1,)), ((), ())), preferred_element_type=F32)
        m_prev = m_sc[...]
        m_next = jnp.maximum(m_prev, jnp.max(s, axis=1, keepdims=True))
        alpha = jnp.exp(m_prev - m_next)
        p = jnp.concatenate(
            [jnp.exp(s[:, j * LANES:(j + 1) * LANES] - m_next) for j in range(tk // LANES)], axis=1)
        v_aug = jnp.concatenate([vc, ones], axis=1)
        pv = jnp.dot(p.astype(BF16), v_aug, preferred_element_type=F32)
        acc_sc[...] = acc_sc[...] * jnp.concatenate([alpha, alpha], axis=1) + pv
        m_sc[...] = m_next
        return carry

    lax.fori_loop(0, n_kc, body, 0)

    lane = lax.broadcasted_iota(jnp.int32, (1, LANES), 1)
    lo_half = lane < HEAD_DIM
    outs = []
    for h in range(N_Q_HEADS):
        a = acc_sc[h * tq:(h + 1) * tq, :]
        outs.append(a[:, :LANES] / a[:, LANES:])
    for j in range(N_Q_HEADS // 2):
        g = (2 * j) // (N_Q_HEADS // N_KV_HEADS)
        a, b = outs[2 * j], outs[2 * j + 1]
        if g == 0:
            b = pltpu.roll(b, HEAD_DIM, 1)
        else:
            a = pltpu.roll(a, HEAD_DIM, 1)
        o_ref[0, :, j * LANES:(j + 1) * LANES] = jnp.where(lo_half, a, b).astype(BF16)


def _attention(q, k, v, tq=128, tk=512):
    B, T, _ = q.shape
    tk = min(tk, T)
    kern = functools.partial(_attn_kernel, tq=tq, tk=tk, n_kc=T // tk)
    return pl.pallas_call(
        kern,
        grid=(B, T // tq),
        in_specs=[pl.BlockSpec((1, tq, N_Q_HEADS * LANES), lambda b, i: (b, i, 0)),
                  pl.BlockSpec((1, T, KV_W), lambda b, i: (b, 0, 0)),
                  pl.BlockSpec((1, T, KV_W), lambda b, i: (b, 0, 0))],
        out_specs=pl.BlockSpec((1, tq, ATTN_W), lambda b, i: (b, i, 0)),
        out_shape=jax.ShapeDtypeStruct((B, T, ATTN_W), BF16),
        scratch_shapes=[pltpu.VMEM((N_Q_HEADS * tq, LANES), F32),
                        pltpu.VMEM((N_Q_HEADS * tq, 2 * LANES), F32)],
        compiler_params=_cparams(("parallel", "parallel")),
        name="attention",
    )(q, k, v)


_GROUP = 8


def _dn_masks():
    r = lax.broadcasted_iota(jnp.int32, (CHUNK, CHUNK), 0)
    c = lax.broadcasted_iota(jnp.int32, (CHUNK, CHUNK), 1)
    return r, c


def _dn_kernel(xq_ref, xk_ref, xv_ref, z_ref, ba_ref, cw_ref, ad_ref, nw_ref, o_ref,
               xpad, qn_sc, kn_sc, vn_sc, u_sc, wq_sc, w2_sc, b_sc, qk_sc, gt_sc, s_sc, of_sc, ob_sc,
               *, T):
    n_chunks = T // CHUNK
    n_groups = n_chunks // _GROUP
    gsz = _GROUP * CHUNK
    h = pl.program_id(1)

    def conv_pass(x_ref, w_idx, dst, mode):
        xpad[0:8, :] = jnp.zeros((8, LANES), F32)
        xpad[T + 8:T + 16, :] = jnp.zeros((8, LANES), F32)
        blk = 256

        def cp(i, c):
            s = pl.multiple_of(i * blk, blk)
            xpad[pl.ds(s + 8, blk), :] = x_ref[0, pl.ds(s, blk), :].astype(F32)
            return c
        lax.fori_loop(0, T // blk, cp, 0)
        w = cw_ref[w_idx]
        L = blk + 16

        def cv(i, c):
            s = pl.multiple_of(i * blk, blk)
            win = xpad[pl.ds(s, L), :]
            acc = None
            for j in range(CONV_W):
                sh = (2 - j) % L
                y = win if sh == 0 else pltpu.roll(win, sh, 0)
                term = y[8:8 + blk, :] * w[j:j + 1, :]
                acc = term if acc is None else acc + term
            y = acc / (1.0 + jnp.exp(-acc))
            if mode == "q":
                y = y * lax.rsqrt(jnp.sum(y * y, axis=-1, keepdims=True) + NORM_EPS) * (DN_HEAD_DIM ** -0.5)
            elif mode == "k":
                y = y * lax.rsqrt(jnp.sum(y * y, axis=-1, keepdims=True) + NORM_EPS)
            dst[pl.ds(s, blk), :] = y
            return c
        lax.fori_loop(0, T // blk, cv, 0)

    conv_pass(xq_ref, 0, qn_sc, "q")
    conv_pass(xk_ref, 1, kn_sc, "k")
    conv_pass(xv_ref, 2, vn_sc, "v")

    r, c = _dn_masks()
    cum_ops = [jnp.concatenate([(c <= r).astype(BF16), (c > r).astype(BF16)], axis=0),
               jnp.concatenate([(c >= r).astype(BF16), (c < r).astype(BF16)], axis=0)]
    r2 = lax.broadcasted_iota(jnp.int32, (CHUNK, LANES), 0)
    c2 = lax.broadcasted_iota(jnp.int32, (CHUNK, LANES), 1)
    lo = c2 < CHUNK
    cc = jnp.where(lo, c2, c2 - CHUNK)
    gs_zero = [jnp.logical_and(lo, r2 <= cc), jnp.logical_and(jnp.logical_not(lo), r2 >= cc)]
    strict_p = jnp.where(lo, r2 - cc, cc - r2) > 0
    incl_p = jnp.where(lo, r2 - cc, cc - r2) >= 0
    eye_p = (cc == r2).astype(F32)
    zeros_c = jnp.zeros((CHUNK, LANES), F32)
    nt = (((1,), (1,)), ((), ()))
    tn = (((0,), (0,)), ((), ()))
    top, bot = slice(0, CHUNK), slice(CHUNK, 2 * CHUNK)
    dirs = range(2)

    def block_diag(p):
        return jnp.concatenate([jnp.where(lo, p, 0.0), jnp.where(lo, 0.0, p)], axis=0).astype(BF16)

    def gates(tok0, d):
        ba = ba_ref[0, 0, pl.ds(tok0, CHUNK), :]
        beta = 1.0 / (1.0 + jnp.exp(-ba[:, d:d + 1]))
        a_in = ba[:, 2 + d:3 + d] + ad_ref[0, 2 + d:3 + d, :][:, 0:1]
        sp = jnp.maximum(a_in, 0.0) + jnp.log(1.0 + jnp.exp(-jnp.abs(a_in)))
        return beta, -jnp.exp(ad_ref[0, d:d + 1, :][:, 0:1]) * sp

    def group_toks(gi):
        hint = (lambda x, m: x) if isinstance(gi, int) else pl.multiple_of
        base_f = hint(gi * gsz, gsz)
        base_b = hint((n_groups - 1 - gi) * gsz, gsz)
        return [(hint(base_f + j * CHUNK, CHUNK), hint(base_b + (_GROUP - 1 - j) * CHUNK, CHUNK))
                for j in range(_GROUP)]

    def prep_group(toks, buf):
        n = len(toks)
        q = [[qn_sc[pl.ds(t[d], CHUNK), :] for d in dirs] for t in toks]
        k = [[kn_sc[pl.ds(t[d], CHUNK), :] for d in dirs] for t in toks]
        v = [[vn_sc[pl.ds(t[d], CHUNK), :] for d in dirs] for t in toks]
        bg = [[gates(t[d], d) for d in dirs] for t in toks]
        cs = []
        for j in range(n):
            row = []
            for d in dirs:
                rhs = jnp.where(gs_zero[d], 0.0, jnp.broadcast_to(bg[j][d][1], (CHUNK, LANES)))
                m = jnp.dot(cum_ops[d], jnp.concatenate(_split_bf16(rhs), axis=1),
                            preferred_element_type=F32)
                row.append(m[:, :LANES] + m[:, LANES:])
            cs.append(row)
        yield
        own = [CHUNK, 0]
        gc = [[cs[j][d][top, own[d]:own[d] + 1] for d in dirs] for j in range(n)]
        grem = [[cs[j][d][bot, own[d]:own[d] + 1] for d in dirs] for j in range(n)]
        decay = [jnp.exp(jnp.where(lo, cs[j][0][top], cs[j][1][top])) for j in range(n)]
        egc = [[jnp.exp(gc[j][d]) for d in dirs] for j in range(n)]
        kb = [[k[j][d] * bg[j][d][0] for d in dirs] for j in range(n)]
        kq = []
        for j in range(n):
            kcat = jnp.concatenate([k[j][0], k[j][1]], axis=0).astype(BF16)
            kq.append([lax.dot_general(jnp.concatenate([kb[j][d], q[j][d]], axis=0).astype(BF16), kcat,
                                       nt, preferred_element_type=F32) for d in dirs])
        yield
        a = [jnp.where(strict_p, jnp.where(lo, kq[j][0][top], kq[j][1][top]) * decay[j], 0.0)
             for j in range(n)]
        for j in range(n):
            qk = jnp.where(incl_p, jnp.where(lo, kq[j][0][bot], kq[j][1][bot]) * decay[j], 0.0)
            qk_sc[buf, pl.ds(j * CHUNK, CHUNK), :] = qk.astype(BF16)
        x = [eye_p - a[j] for j in range(n)]
        p = a
        pbd = [block_diag(p[j]) for j in range(n)]
        for _ in range(5):
            p = [jnp.dot(p[j].astype(BF16), pbd[j], preferred_element_type=F32) for j in range(n)]
            yield
            pbd = [block_diag(p[j]) for j in range(n)]
            x = [x[j] + jnp.dot(x[j].astype(BF16), pbd[j], preferred_element_type=F32) for j in range(n)]
            yield
        uw = []
        for j in range(n):
            r_f = jnp.concatenate([v[j][0] * bg[j][0][0], kb[j][0] * egc[j][0], zeros_c, zeros_c], axis=1)
            r_b = jnp.concatenate([zeros_c, zeros_c, v[j][1] * bg[j][1][0], kb[j][1] * egc[j][1]], axis=1)
            uw.append(jnp.dot(x[j].astype(BF16), jnp.concatenate([r_f, r_b], axis=0).astype(BF16),
                              preferred_element_type=F32))
        yield
        for j in range(n):
            for d in dirs:
                u = uw[j][:, 2 * d * LANES:(2 * d + 1) * LANES]
                w = uw[j][:, (2 * d + 1) * LANES:(2 * d + 2) * LANES]
                kd = (k[j][d] * jnp.exp(grem[j][d])).astype(BF16)
                w2b = lax.dot_general(kd, jnp.concatenate([w, u], axis=1).astype(BF16), tn,
                                      preferred_element_type=F32)
                blk = pl.ds(j * 2 * CHUNK, 2 * CHUNK)
                bd = buf * 2 + d
                u_sc[bd, pl.ds(j * CHUNK, CHUNK), :] = u
                wq_sc[bd, blk, :] = jnp.concatenate([w, q[j][d] * egc[j][d]], axis=0).astype(BF16)
                w2_sc[bd, blk, :] = w2b[:, :LANES].astype(BF16)
                b_sc[bd, blk, :] = w2b[:, LANES:]
                tot = gc[j][d][CHUNK - 1:CHUNK, :] if d == 0 else gc[j][d][0:1, :]
                gt_sc[bd, pl.ds(j * 8, 8), :] = jnp.broadcast_to(jnp.exp(tot), (8, LANES))
        yield

    def recur_step(j, buf, toks, s):
        blk = pl.ds(j * 2 * CHUNK, 2 * CHUNK)
        sb = [s[d].astype(BF16) for d in dirs]
        ws = [jnp.dot(w2_sc[buf * 2 + d, blk, :], sb[d], preferred_element_type=F32) for d in dirs]
        sw = [jnp.dot(wq_sc[buf * 2 + d, blk, :], sb[d], preferred_element_type=F32) for d in dirs]
        s = [s[d] * gt_sc[buf * 2 + d, pl.ds(j * 8, 1), :] + b_sc[buf * 2 + d, blk, :] - ws[d] for d in dirs]
        vn = [u_sc[buf * 2 + d, pl.ds(j * CHUNK, CHUNK), :] - sw[d][top] for d in dirs]
        vbd = jnp.concatenate([jnp.concatenate([vn[0], zeros_c], axis=1),
                               jnp.concatenate([zeros_c, vn[1]], axis=1)], axis=0).astype(BF16)
        oo = jnp.dot(qk_sc[buf, pl.ds(j * CHUNK, CHUNK), :], vbd, preferred_element_type=F32)
        of_sc[pl.ds(toks[j][0], CHUNK), :] = sw[0][bot] + oo[:, :LANES]
        ob_sc[pl.ds(toks[j][1], CHUNK), :] = sw[1][bot] + oo[:, LANES:]
        return s

    n_stage = 14
    per_step = [-(-n_stage * (j + 1) // _GROUP) - (-(-n_stage * j // _GROUP)) for j in range(_GROUP)]

    for _ in prep_group(group_toks(0), 0):
        pass
    s_sc[...] = jnp.zeros(s_sc.shape, F32)

    def group_body(gi, carry):
        buf = gi % 2
        toks = group_toks(gi)
        nxt = prep_group(group_toks(gi + 1), 1 - buf)
        s = [s_sc[0], s_sc[1]]
        for j in range(_GROUP):
            for _ in range(per_step[j]):
                next(nxt)
            s = recur_step(j, buf, toks, s)
        s_sc[0] = s[0]
        s_sc[1] = s[1]
        return carry

    lax.fori_loop(0, n_groups - 1, group_body, 0)
    last = n_groups - 1
    s = [s_sc[0], s_sc[1]]
    for j in range(_GROUP):
        s = recur_step(j, last % 2, group_toks(last), s)

    blk = 256

    def fin(i, carry):
        s = pl.multiple_of(i * blk, blk)
        o = of_sc[pl.ds(s, blk), :] + ob_sc[pl.ds(s, blk), :]
        o = o * lax.rsqrt(jnp.mean(o * o, axis=-1, keepdims=True) + NORM_EPS) * nw_ref[...]
        z = z_ref[0, pl.ds(s, blk), :].astype(F32)
        o_ref[0, pl.ds(s, blk), :] = (o * (z / (1.0 + jnp.exp(-z)))).astype(BF16)
        return carry
    lax.fori_loop(0, T // blk, fin, 0)


def _deltanet(dn, z, ba, conv_w, A_log, dt_bias, dn_norm_w):
    B, T, _ = dn.shape
    H = N_DN_HEADS
    ba_t = ba.reshape(B, T, 2, 2, H).transpose(0, 4, 1, 2, 3).reshape(B, H, T, 4)
    cw = jnp.pad(conv_w, ((0, 8 - CONV_W), (0, 0))).reshape(8, 3, H, DN_HEAD_DIM).transpose(2, 1, 0, 3)
    cw = cw.reshape(H * 3, 8, DN_HEAD_DIM)
    ad = jnp.concatenate([A_log, dt_bias], axis=0).T
    ad = jnp.broadcast_to(ad[:, :, None], (H, 4, LANES))
    ad = jnp.pad(ad, ((0, 0), (0, 4), (0, 0)))
    gsz = _GROUP * CHUNK
    kern = functools.partial(_dn_kernel, T=T)
    col = lambda off: pl.BlockSpec((1, T, DN_HEAD_DIM), lambda b, h, off=off: (b, 0, off + h))
    return pl.pallas_call(
        kern,
        grid=(B, H),
        in_specs=[col(0), col(H), col(2 * H),
                  pl.BlockSpec((1, T, DN_HEAD_DIM), lambda b, h: (b, 0, h)),
                  pl.BlockSpec((1, 1, T, 4), lambda b, h: (b, h, 0, 0)),
                  pl.BlockSpec((3, 8, DN_HEAD_DIM), lambda b, h: (h, 0, 0)),
                  pl.BlockSpec((1, 8, LANES), lambda b, h: (h, 0, 0)),
                  pl.BlockSpec((1, DN_HEAD_DIM), lambda b, h: (0, 0))],
        out_specs=pl.BlockSpec((1, T, DN_HEAD_DIM), lambda b, h: (b, 0, h)),
        out_shape=jax.ShapeDtypeStruct((B, T, DN_W), BF16),
        scratch_shapes=[pltpu.VMEM((T + 16, LANES), F32),
                        pltpu.VMEM((T, LANES), F32),
                        pltpu.VMEM((T, LANES), F32),
                        pltpu.VMEM((T, LANES), F32),
                        pltpu.VMEM((4, gsz, LANES), F32),
                        pltpu.VMEM((4, 2 * gsz, LANES), BF16),
                        pltpu.VMEM((4, 2 * gsz, LANES), BF16),
                        pltpu.VMEM((4, 2 * gsz, LANES), F32),
                        pltpu.VMEM((2, gsz, LANES), BF16),
                        pltpu.VMEM((4, _GROUP * 8, LANES), F32),
                        pltpu.VMEM((2, DN_HEAD_DIM, DN_HEAD_DIM), F32),
                        pltpu.VMEM((T, LANES), F32),
                        pltpu.VMEM((T, LANES), F32)],
        compiler_params=_cparams(("parallel", "parallel")),
        name="deltanet",
    )(dn, dn, dn, z, ba_t, cw, ad, dn_norm_w[None, :])


def _outproj_kernel(x_ref, at_ref, dn_ref, wo_ref, n2w_ref, rw_ref, rb_ref,
                    h_ref, h2_ref, idx_ref, gate_ref):
    mix = (jnp.dot(at_ref[...], wo_ref[0:ATTN_W, :], preferred_element_type=F32)
           + jnp.dot(dn_ref[...], wo_ref[ATTN_W:, :], preferred_element_type=F32))
    hres = x_ref[...] + mix
    h_ref[...] = hres
    h2 = hres * lax.rsqrt(jnp.mean(hres * hres, axis=-1, keepdims=True) + NORM_EPS) * n2w_ref[...]
    h2_ref[...] = h2
    logits = jnp.dot(h2, rw_ref[...], preferred_element_type=F32,
                     precision=lax.Precision.HIGHEST) + rb_ref[...]
    lane = lax.broadcasted_iota(jnp.int32, logits.shape, 1).astype(F32)
    vals = []
    idxs = []
    l = logits
    for _ in range(TOP_K):
        m = jnp.max(l, axis=-1, keepdims=True)
        i = jnp.min(jnp.where(l == m, lane, float(LANES)), axis=-1, keepdims=True)
        vals.append(m)
        idxs.append(i)
        l = jnp.where(lane == i, -jnp.inf, l)
    es = [jnp.exp(v - vals[0]) for v in vals]
    den = es[0] + es[1] + es[2] + es[3]
    idx_ref[...] = jnp.concatenate(idxs, axis=1).astype(jnp.int32)
    gate_ref[...] = jnp.concatenate([e / den for e in es], axis=1)


def _outproj_router(x2d, attn2d, dn2d, w_out_b, norm2_w, router_w, router_b, tm=512):
    n = x2d.shape[0]
    rw = jnp.pad(router_w, ((0, 0), (0, LANES - N_EXPERTS)))
    rb = jnp.pad(router_b, (0, LANES - N_EXPERTS), constant_values=-jnp.inf)[None, :]
    const = lambda shape: pl.BlockSpec(shape, lambda i: (0,) * len(shape))
    row = lambda w: pl.BlockSpec((tm, w), lambda i: (i, 0))
    return pl.pallas_call(
        _outproj_kernel,
        grid=(n // tm,),
        in_specs=[row(D_MODEL), row(ATTN_W), row(DN_W), const((D_MODEL, D_MODEL)),
                  const((1, D_MODEL)), const((D_MODEL, LANES)), const((1, LANES))],
        out_specs=[row(D_MODEL), row(D_MODEL), row(TOP_K), row(TOP_K)],
        out_shape=[jax.ShapeDtypeStruct((n, D_MODEL), F32),
                   jax.ShapeDtypeStruct((n, D_MODEL), F32),
                   jax.ShapeDtypeStruct((n, TOP_K), jnp.int32),
                   jax.ShapeDtypeStruct((n, TOP_K), F32)],
        compiler_params=_cparams(("parallel",)),
        name="outproj_router",
    )(x2d, attn2d, dn2d, w_out_b, norm2_w[None, :], rw, rb)


_BM = 256


def _moe_kernel(be_ref, nu_ref, cnt_ref, tok_ref, tokn_ref, dst_ref, h2_hbm, w1_ref, b1_ref, w2_ref,
                b2_ref, yk_hbm, xbuf, ybuf, gsem, ssem):
    i = pl.program_id(0)
    n_used = nu_ref[0]
    slot = i % 2

    def gather(idx_ref, sl):
        def body(rr, c):
            t = idx_ref[0, 0, rr]
            pltpu.make_async_copy(h2_hbm.at[pl.ds(t, 1)], xbuf.at[sl, pl.ds(rr, 1)], gsem.at[sl]).start()
            return c
        lax.fori_loop(0, _BM, body, 0, unroll=8)

    def wait_gather(sl):
        pltpu.make_async_copy(h2_hbm.at[pl.ds(0, _BM)], xbuf.at[sl], gsem.at[sl]).wait()

    def row_scatter(sl, rr):
        dd = dst_ref[0, 0, rr]
        return pltpu.make_async_copy(ybuf.at[sl, pl.ds(rr, 1)], yk_hbm.at[pl.ds(dd, 1)], ssem.at[sl])

    def start_scatter(sl, n_valid):
        def body(rr, c):
            row_scatter(sl, rr).start()
            return c

        @pl.when(n_valid == _BM)
        def _():
            lax.fori_loop(0, _BM, body, 0, unroll=8)

        @pl.when(n_valid < _BM)
        def _():
            lax.fori_loop(0, n_valid, body, 0)

    def wait_scatter(sl, n_valid):
        @pl.when(n_valid == _BM)
        def _():
            pltpu.make_async_copy(ybuf.at[sl], yk_hbm.at[pl.ds(0, _BM)], ssem.at[sl]).wait()

        @pl.when(n_valid < _BM)
        def _():
            def body(rr, c):
                pltpu.make_async_copy(ybuf.at[sl, pl.ds(0, 1)], yk_hbm.at[pl.ds(0, 1)], ssem.at[sl]).wait()
                return c
            lax.fori_loop(0, n_valid, body, 0)

    @pl.when(i == 0)
    def _():
        gather(tok_ref, 0)

    @pl.when(i + 1 < n_used)
    def _():
        gather(tokn_ref, 1 - slot)

    @pl.when(i < n_used)
    def _():
        wait_gather(slot)

        @pl.when(i >= 2)
        def _():
            wait_scatter(slot, cnt_ref[jnp.maximum(i - 2, 0)])

        x = xbuf[slot].astype(BF16)
        hh = jnp.dot(x, w1_ref[0], preferred_element_type=F32) + b1_ref[0]
        gate = jnp.minimum(hh[:, :D_FF], SWIGLU_LIMIT)
        up = jnp.clip(hh[:, D_FF:], -SWIGLU_LIMIT, SWIGLU_LIMIT)
        act = gate * (1.0 / (1.0 + jnp.exp(-SWIGLU_ALPHA * gate))) * (up + 1.0)
        ybuf[slot] = jnp.dot(act.astype(BF16), w2_ref[0], preferred_element_type=F32) + b2_ref[0]
        start_scatter(slot, cnt_ref[i])

        @pl.when(i == n_used - 1)
        def _():
            wait_scatter(slot, cnt_ref[i])

            @pl.when(i >= 1)
            def _():
                wait_scatter(1 - slot, cnt_ref[jnp.maximum(i - 1, 0)])


def _moe_blocks(n_tok):
    n_assign = n_tok * TOP_K
    return n_assign // _BM + N_EXPERTS


def _routing_plan(idx):
    n_tok = idx.shape[0]
    n_assign = n_tok * TOP_K
    nb = _moe_blocks(n_tok)
    n_rows = nb * _BM
    flat_e = idx.reshape(-1)
    order = jnp.argsort(flat_e, stable=True).astype(jnp.int32)
    counts = jnp.sum(flat_e[:, None] == jnp.arange(N_EXPERTS, dtype=jnp.int32)[None, :], axis=0,
                     dtype=jnp.int32)
    padded = (counts + _BM - 1) // _BM * _BM
    start = jnp.cumsum(counts) - counts
    pad_end = jnp.cumsum(padded)
    pad_start = pad_end - padded
    n_used = (pad_end[-1] // _BM).astype(jnp.int32)
    blk = jnp.minimum(jnp.arange(nb, dtype=jnp.int32), n_used - 1) * _BM
    block_e = jnp.minimum(jnp.sum(pad_end[None, :] <= blk[:, None], axis=1), N_EXPERTS - 1).astype(jnp.int32)
    cnt = jnp.clip(pad_start[block_e] + counts[block_e] - blk, 0, _BM).astype(jnp.int32)
    rank = blk[:, None] - pad_start[block_e][:, None] + jnp.arange(_BM, dtype=jnp.int32)[None, :]
    valid = rank < counts[block_e][:, None]
    src = jnp.clip(start[block_e][:, None] + rank, 0, n_assign - 1)
    assign = jnp.where(valid, order[src], 0)
    row_tok = assign // TOP_K
    row_dst = (assign % TOP_K) * n_tok + row_tok
    return (block_e, n_used.reshape(1), cnt, row_tok.reshape(nb, 1, _BM), row_dst.reshape(nb, 1, _BM))


def _moe(h2, idx, w1_b, b1, w2_b, b2):
    n_tok = h2.shape[0]
    n_assign = n_tok * TOP_K
    nb = _moe_blocks(n_tok)
    block_e, n_used, cnt, row_tok, row_dst = _routing_plan(idx)
    smem_blk = lambda f: pl.BlockSpec((1, 1, _BM), f, memory_space=pltpu.SMEM)
    grid_spec = pltpu.PrefetchScalarGridSpec(
        num_scalar_prefetch=3,
        grid=(nb,),
        in_specs=[smem_blk(lambda i, be, nu, ct: (i, 0, 0)),
                  smem_blk(lambda i, be, nu, ct: (jnp.minimum(i + 1, nb - 1), 0, 0)),
                  smem_blk(lambda i, be, nu, ct: (i, 0, 0)),
                  pl.BlockSpec(memory_space=pl.ANY),
                  pl.BlockSpec((1, D_MODEL, 2 * D_FF), lambda i, be, nu, ct: (be[i], 0, 0)),
                  pl.BlockSpec((1, 1, 2 * D_FF), lambda i, be, nu, ct: (be[i], 0, 0)),
                  pl.BlockSpec((1, D_FF, D_MODEL), lambda i, be, nu, ct: (be[i], 0, 0)),
                  pl.BlockSpec((1, 1, D_MODEL), lambda i, be, nu, ct: (be[i], 0, 0))],
        out_specs=pl.BlockSpec(memory_space=pl.ANY),
        scratch_shapes=[pltpu.VMEM((2, _BM, D_MODEL), F32),
                        pltpu.VMEM((2, _BM, D_MODEL), F32),
                        pltpu.SemaphoreType.DMA((2,)),
                        pltpu.SemaphoreType.DMA((2,))],
    )
    return pl.pallas_call(
        _moe_kernel,
        grid_spec=grid_spec,
        out_shape=jax.ShapeDtypeStruct((n_assign, D_MODEL), F32),
        compiler_params=_cparams(("arbitrary",)),
        name="moe_experts",
    )(block_e, n_used, cnt, row_tok, row_tok, row_dst, h2, w1_b, b1[:, None, :], w2_b, b2[:, None, :])


def _combine_kernel(h_ref, y0_ref, y1_ref, y2_ref, y3_ref, g_ref, o_ref):
    g = g_ref[...]
    acc = h_ref[...]
    for k, y_ref in enumerate((y0_ref, y1_ref, y2_ref, y3_ref)):
        acc = acc + g[:, k:k + 1] * y_ref[0]
    o_ref[...] = acc


def _combine(h, yk, gates, tm=256):
    n = h.shape[0]
    ykv = yk.reshape(TOP_K, n, D_MODEL)
    row = lambda w: pl.BlockSpec((tm, w), lambda i: (i, 0))
    ysp = lambda k: pl.BlockSpec((1, tm, D_MODEL), lambda i, k=k: (k, i, 0))
    return pl.pallas_call(
        _combine_kernel,
        grid=(n // tm,),
        in_specs=[row(D_MODEL), ysp(0), ysp(1), ysp(2), ysp(3), row(TOP_K)],
        out_specs=row(D_MODEL),
        out_shape=jax.ShapeDtypeStruct((n, D_MODEL), F32),
        compiler_params=_cparams(("parallel",)),
        name="moe_combine",
    )(h, ykv, ykv, ykv, ykv, gates)


def _layer(x, p):
    B, T, D = x.shape
    x2d = x.reshape(B * T, D)
    q, k, v, dn, z, ba = _inproj(x2d, T, p["norm1_w"], p["w_in"], p["q_norm_w"], p["k_norm_w"])
    attn = _attention(q.reshape(B, T, -1), k.reshape(B, T, -1), v.reshape(B, T, -1))
    dno = _deltanet(dn.reshape(B, T, -1), z.reshape(B, T, -1), ba.reshape(B, T, -1),
                    p["conv_w"], p["A_log"], p["dt_bias"], p["dn_norm_w"])
    h, h2, idx, gates = _outproj_router(x2d, attn.reshape(B * T, -1), dno.reshape(B * T, -1),
                                        p["w_out"], p["norm2_w"], p["router_w"], p["router_b"])
    yk = _moe(h2, idx, p["w1"], p["b1"], p["w2"], p["b2"])
    y = _combine(h, yk, gates)
    return y.reshape(B, T, D)


def kernel(x_prompt, x_sample, norm1_w, w_in, conv_w, q_norm_w, k_norm_w, A_log, dt_bias, dn_norm_w,
           w_out, norm2_w, router_w, router_b, w1, b1, w2, b2):
    l = 0
    p = {
        "norm1_w": norm1_w[l],
        "w_in": jnp.pad(w_in[l], ((0, 0), (0, PROJ_PAD - w_in.shape[-1]))).astype(BF16),
        "conv_w": conv_w[l], "q_norm_w": q_norm_w[l], "k_norm_w": k_norm_w[l],
        "A_log": A_log[l], "dt_bias": dt_bias[l], "dn_norm_w": dn_norm_w[l],
        "w_out": w_out[l].astype(BF16), "norm2_w": norm2_w[l],
        "router_w": router_w[l], "router_b": router_b[l],
        "w1": w1[l].astype(BF16), "b1": b1[l], "w2": w2[l].astype(BF16), "b2": b2[l],
    }
    return (_layer(x_prompt, p), _layer(x_sample, p))
```

```python
import functools
import math

import numpy as np
import jax
import jax.numpy as jnp
from jax import lax
from jax.experimental import pallas as pl
from jax.experimental.pallas import tpu as pltpu

F32 = jnp.float32
BF16 = jnp.bfloat16

D_MODEL = 1024
ATTN_W = 512
HEAD_DIM = 64
N_Q_HEADS = 8
N_KV_HEADS = 2
KV_W = 128
DN_W = 512
DN_HEAD_DIM = 128
N_DN_HEADS = 4
CONV_W = 5
CHUNK = 64
GRID_W = 64
ROPE_THETA = 10000.0
N_EXPERTS = 32
TOP_K = 4
D_FF = 1024
SWIGLU_LIMIT = 7.0
SWIGLU_ALPHA = 1.702
NORM_EPS = 1e-6

LANES = 128
PROJ_PAD = 2944
VMEM_LIMIT = 56 * 1024 * 1024

_C_Q = 0
_C_KV = ATTN_W
_C_DN = _C_KV + 2 * KV_W
_C_Z = _C_DN + 3 * DN_W
_C_BA = _C_Z + DN_W


def _cparams(sem):
    return pltpu.CompilerParams(dimension_semantics=sem, vmem_limit_bytes=VMEM_LIMIT)


def _rope_tables(T):
    t = np.arange(T)
    row = (t // GRID_W).astype(np.float64)
    col = (t % GRID_W).astype(np.float64)
    half = HEAD_DIM // 2
    freqs = ROPE_THETA ** (-np.arange(0, half, 2, dtype=np.float64) / half)
    lane = np.arange(LANES)
    d = lane % HEAD_DIM
    use_col = (d // half) == 1
    f = d % (half // 2)
    first = (d % half) < (half // 2)
    pos = np.where(use_col[None, :], col[:, None], row[:, None])
    ang = pos * freqs[f][None, :]
    cos = np.cos(ang)
    sin = np.sin(ang)
    sin_a = np.where(first[None, :], -sin, 0.0)
    sin_b = np.where(first[None, :], 0.0, sin)
    return (jnp.asarray(cos, F32), jnp.asarray(sin_a, F32), jnp.asarray(sin_b, F32))


def _head_mean_matrix():
    lane = np.arange(LANES)
    m = (lane[:, None] // HEAD_DIM == lane[None, :] // HEAD_DIM).astype(np.float32) / HEAD_DIM
    return jnp.asarray(m, BF16)


def _split_bf16(x):
    hi = x.astype(BF16)
    lo = (x - hi.astype(F32)).astype(BF16)
    return hi, lo


def _inproj_kernel(x_ref, n1w_ref, w_ref, qnw_ref, knw_ref, cos_ref, sa_ref, sb_ref, hm_ref,
                   q_ref, k_ref, v_ref, dn_ref, z_ref, ba_ref):
    x = x_ref[...]
    ms = jnp.mean(x * x, axis=-1, keepdims=True)
    hn = (x * lax.rsqrt(ms + NORM_EPS) * n1w_ref[...]).astype(BF16)

    def proj(a, b):
        return jnp.dot(hn, w_ref[:, a:b], preferred_element_type=F32)

    hm = hm_ref[...]
    cos = cos_ref[...]
    sa = sa_ref[...]
    sb = sb_ref[...]
    lane = lax.broadcasted_iota(jnp.int32, (1, LANES), 1)
    lo_half = lane < HEAD_DIM

    def norm_rope(xs, w):
        hi, lo = _split_bf16(xs * xs)
        msq = (jnp.dot(hi, hm, preferred_element_type=F32)
               + jnp.dot(lo, hm, preferred_element_type=F32))
        xn = xs * lax.rsqrt(msq + NORM_EPS) * w
        return (xn * cos + pltpu.roll(xn, LANES - 16, 1) * sa + pltpu.roll(xn, 16, 1) * sb)

    q = proj(_C_Q, _C_Q + ATTN_W)
    scale = HEAD_DIM ** -0.5 * math.log2(math.e)
    for j in range(ATTN_W // LANES):
        qr = norm_rope(q[:, j * LANES:(j + 1) * LANES], qnw_ref[...] * scale)
        qs = pltpu.roll(qr, HEAD_DIM, 1)
        g = (2 * j) // (N_Q_HEADS // N_KV_HEADS)
        keep = lo_half if g == 0 else jnp.logical_not(lo_half)
        h0 = qr if g == 0 else qs
        h1 = qs if g == 0 else qr
        q_ref[:, (2 * j) * LANES:(2 * j + 1) * LANES] = jnp.where(keep, h0, 0.0).astype(BF16)
        q_ref[:, (2 * j + 1) * LANES:(2 * j + 2) * LANES] = jnp.where(keep, h1, 0.0).astype(BF16)

    kv = proj(_C_KV, _C_KV + 2 * KV_W)
    k_ref[...] = norm_rope(kv[:, :KV_W], knw_ref[...]).astype(BF16)
    v_ref[...] = kv[:, KV_W:].astype(BF16)
    dn_ref[...] = proj(_C_DN, _C_DN + 3 * DN_W).astype(BF16)
    z_ref[...] = proj(_C_Z, _C_Z + DN_W).astype(BF16)
    ba_ref[...] = proj(_C_BA, _C_BA + LANES)[:, :4 * N_DN_HEADS]


def _inproj(x2d, T, norm1_w, w_in_b, q_norm_w, k_norm_w, tm=512):
    n = x2d.shape[0]
    cos, sa, sb = _rope_tables(T)
    tm = min(tm, T)
    tpb = T // tm
    tab_spec = pl.BlockSpec((tm, LANES), lambda i: (i % tpb, 0))
    const = lambda shape: pl.BlockSpec(shape, lambda i: (0,) * len(shape))
    row = lambda w: pl.BlockSpec((tm, w), lambda i: (i, 0))
    qnw = jnp.tile(q_norm_w, LANES // HEAD_DIM)[None, :]
    knw = jnp.tile(k_norm_w, LANES // HEAD_DIM)[None, :]
    return pl.pallas_call(
        _inproj_kernel,
        grid=(n // tm,),
        in_specs=[row(D_MODEL), const((1, D_MODEL)), const((D_MODEL, PROJ_PAD)),
                  const((1, LANES)), const((1, LANES)), tab_spec, tab_spec, tab_spec,
                  const((LANES, LANES))],
        out_specs=[row(N_Q_HEADS * LANES), row(KV_W), row(KV_W), row(3 * DN_W), row(DN_W),
                   row(4 * N_DN_HEADS)],
        out_shape=[jax.ShapeDtypeStruct((n, N_Q_HEADS * LANES), BF16),
                   jax.ShapeDtypeStruct((n, KV_W), BF16),
                   jax.ShapeDtypeStruct((n, KV_W), BF16),
                   jax.ShapeDtypeStruct((n, 3 * DN_W), BF16),
                   jax.ShapeDtypeStruct((n, DN_W), BF16),
                   jax.ShapeDtypeStruct((n, 4 * N_DN_HEADS), F32)],
        compiler_params=_cparams(("parallel",)),
        name="inproj",
    )(x2d, norm1_w[None, :], w_in_b, qnw, knw, cos, sa, sb, _head_mean_matrix())


_ATTN_PARTS = 4


def _attn_kernel(q_ref, k_ref, v_ref, o_ref, m_sc, acc_sc, *, tq, tk, n_kc):
    hp = N_Q_HEADS // _ATTN_PARTS
    rows = hp * tq
    qs = [jnp.concatenate([q_ref[0, :, h * LANES:(h + 1) * LANES] for h in range(u * hp, (u + 1) * hp)],
                          axis=0) for u in range(_ATTN_PARTS)]
    m_sc[...] = jnp.full(m_sc.shape, -jnp.inf, F32)
    acc_sc[...] = jnp.zeros(acc_sc.shape, F32)
    ones = jnp.ones((tk, LANES), BF16)

    def scores(start, u):
        kc = k_ref[0, pl.ds(start, tk), :]
        return lax.dot_general(qs[u], kc, (((1,), (1,)), ((), ())), preferred_element_type=F32)

    def softmax_pv(s, start, u):
        r = slice(u * rows, (u + 1) * rows)
        m_prev = m_sc[r, :]
        m_next = jnp.maximum(m_prev, jnp.max(s, axis=1, keepdims=True))
        alpha = jnp.exp2(m_prev - m_next)
        p = jnp.concatenate(
            [jnp.exp2(s[:, j * LANES:(j + 1) * LANES] - m_next) for j in range(tk // LANES)], axis=1)
        v_aug = jnp.concatenate([v_ref[0, pl.ds(start, tk), :], ones], axis=1)
        pv = jnp.dot(p.astype(BF16), v_aug, preferred_element_type=F32)
        acc_sc[r, :] = acc_sc[r, :] * jnp.concatenate([alpha, alpha], axis=1) + pv
        m_sc[r, :] = m_next

    unroll = 2 if n_kc % 2 == 0 else 1

    def body(c, carry):
        units = [(pl.multiple_of((c * unroll + cc) * tk, tk), u)
                 for cc in range(unroll) for u in range(_ATTN_PARTS)]
        s = {i: scores(*units[i]) for i in range(min(2, len(units)))}
        for i, (start, u) in enumerate(units):
            softmax_pv(s.pop(i), start, u)
            if i + 2 < len(units):
                s[i + 2] = scores(*units[i + 2])
        return carry

    lax.fori_loop(0, n_kc // unroll, body, 0)

    lane = lax.broadcasted_iota(jnp.int32, (1, LANES), 1)
    lo_half = lane < HEAD_DIM
    outs = []
    for h in range(N_Q_HEADS):
        a = acc_sc[h * tq:(h + 1) * tq, :]
        outs.append(a[:, :LANES] / a[:, LANES:])
    for j in range(N_Q_HEADS // 2):
        g = (2 * j) // (N_Q_HEADS // N_KV_HEADS)
        a, b = outs[2 * j], outs[2 * j + 1]
        if g == 0:
            b = pltpu.roll(b, HEAD_DIM, 1)
        else:
            a = pltpu.roll(a, HEAD_DIM, 1)
        o_ref[0, :, j * LANES:(j + 1) * LANES] = jnp.where(lo_half, a, b).astype(BF16)


def _attention(q, k, v, tq=128, tk=512):
    B, T, _ = q.shape
    tk = min(tk, T)
    kern = functools.partial(_attn_kernel, tq=tq, tk=tk, n_kc=T // tk)
    return pl.pallas_call(
        kern,
        grid=(B, T // tq),
        in_specs=[pl.BlockSpec((1, tq, N_Q_HEADS * LANES), lambda b, i: (b, i, 0)),
                  pl.BlockSpec((1, T, KV_W), lambda b, i: (b, 0, 0)),
                  pl.BlockSpec((1, T, KV_W), lambda b, i: (b, 0, 0))],
        out_specs=pl.BlockSpec((1, tq, ATTN_W), lambda b, i: (b, i, 0)),
        out_shape=jax.ShapeDtypeStruct((B, T, ATTN_W), BF16),
        scratch_shapes=[pltpu.VMEM((N_Q_HEADS * tq, LANES), F32),
                        pltpu.VMEM((N_Q_HEADS * tq, 2 * LANES), F32)],
        compiler_params=_cparams(("parallel", "parallel")),
        name="attention",
    )(q, k, v)


_GROUP = 8


def _dn_masks():
    r = lax.broadcasted_iota(jnp.int32, (CHUNK, CHUNK), 0)
    c = lax.broadcasted_iota(jnp.int32, (CHUNK, CHUNK), 1)
    return r, c


def _dn_kernel(xq_ref, xk_ref, xv_ref, z_ref, ba_ref, cw_ref, ad_ref, nw_ref, o_ref,
               xpad, qn_sc, kn_sc, vn_sc, u_sc, wq_sc, w2_sc, b_sc, qk_sc, gt_sc, s_sc, of_sc, ob_sc,
               *, T):
    n_chunks = T // CHUNK
    n_groups = n_chunks // _GROUP
    gsz = _GROUP * CHUNK
    h = pl.program_id(1)

    def conv_pass(x_ref, w_idx, dst, mode):
        xpad[0:8, :] = jnp.zeros((8, LANES), F32)
        xpad[T + 8:T + 16, :] = jnp.zeros((8, LANES), F32)
        blk = 256

        def cp(i, c):
            s = pl.multiple_of(i * blk, blk)
            xpad[pl.ds(s + 8, blk), :] = x_ref[0, pl.ds(s, blk), :].astype(F32)
            return c
        lax.fori_loop(0, T // blk, cp, 0)
        w = cw_ref[w_idx]
        L = blk + 16

        def cv(i, c):
            s = pl.multiple_of(i * blk, blk)
            win = xpad[pl.ds(s, L), :]
            acc = None
            for j in range(CONV_W):
                sh = (2 - j) % L
                y = win if sh == 0 else pltpu.roll(win, sh, 0)
                term = y[8:8 + blk, :] * w[j:j + 1, :]
                acc = term if acc is None else acc + term
            y = acc / (1.0 + jnp.exp(-acc))
            if mode == "q":
                y = y * lax.rsqrt(jnp.sum(y * y, axis=-1, keepdims=True) + NORM_EPS) * (DN_HEAD_DIM ** -0.5)
            elif mode == "k":
                y = y * lax.rsqrt(jnp.sum(y * y, axis=-1, keepdims=True) + NORM_EPS)
            dst[pl.ds(s, blk), :] = y
            return c
        lax.fori_loop(0, T // blk, cv, 0)

    conv_pass(xq_ref, 0, qn_sc, "q")
    conv_pass(xk_ref, 1, kn_sc, "k")
    conv_pass(xv_ref, 2, vn_sc, "v")

    r, c = _dn_masks()
    cum_ops = [jnp.concatenate([(c <= r).astype(BF16), (c > r).astype(BF16)], axis=0),
               jnp.concatenate([(c >= r).astype(BF16), (c < r).astype(BF16)], axis=0)]
    r2 = lax.broadcasted_iota(jnp.int32, (CHUNK, LANES), 0)
    c2 = lax.broadcasted_iota(jnp.int32, (CHUNK, LANES), 1)
    lo = c2 < CHUNK
    cc = jnp.where(lo, c2, c2 - CHUNK)
    gs_zero = [jnp.logical_and(lo, r2 <= cc), jnp.logical_and(jnp.logical_not(lo), r2 >= cc)]
    strict_p = jnp.where(lo, r2 - cc, cc - r2) > 0
    incl_p = jnp.where(lo, r2 - cc, cc - r2) >= 0
    eye_p = (cc == r2).astype(F32)
    zeros_c = jnp.zeros((CHUNK, LANES), F32)
    nt = (((1,), (1,)), ((), ()))
    tn = (((0,), (0,)), ((), ()))
    top, bot = slice(0, CHUNK), slice(CHUNK, 2 * CHUNK)
    dirs = range(2)

    def block_diag(p):
        return jnp.concatenate([jnp.where(lo, p, 0.0), jnp.where(lo, 0.0, p)], axis=0).astype(BF16)

    def gates(tok0, d):
        ba = ba_ref[0, 0, pl.ds(tok0, CHUNK), :]
        beta = 1.0 / (1.0 + jnp.exp(-ba[:, d:d + 1]))
        a_in = ba[:, 2 + d:3 + d] + ad_ref[0, 2 + d:3 + d, :][:, 0:1]
        sp = jnp.maximum(a_in, 0.0) + jnp.log(1.0 + jnp.exp(-jnp.abs(a_in)))
        return beta, -jnp.exp(ad_ref[0, d:d + 1, :][:, 0:1]) * sp

    def group_toks(gi):
        hint = (lambda x, m: x) if isinstance(gi, int) else pl.multiple_of
        base_f = hint(gi * gsz, gsz)
        base_b = hint((n_groups - 1 - gi) * gsz, gsz)
        return [(hint(base_f + j * CHUNK, CHUNK), hint(base_b + (_GROUP - 1 - j) * CHUNK, CHUNK))
                for j in range(_GROUP)]

    def prep_group(toks, buf):
        n = len(toks)
        q = [[qn_sc[pl.ds(t[d], CHUNK), :] for d in dirs] for t in toks]
        k = [[kn_sc[pl.ds(t[d], CHUNK), :] for d in dirs] for t in toks]
        v = [[vn_sc[pl.ds(t[d], CHUNK), :] for d in dirs] for t in toks]
        bg = [[gates(t[d], d) for d in dirs] for t in toks]
        cs = []
        for j in range(n):
            row = []
            for d in dirs:
                rhs = jnp.where(gs_zero[d], 0.0, jnp.broadcast_to(bg[j][d][1], (CHUNK, LANES)))
                m = jnp.dot(cum_ops[d], jnp.concatenate(_split_bf16(rhs), axis=1),
                            preferred_element_type=F32)
                row.append(m[:, :LANES] + m[:, LANES:])
            cs.append(row)
        yield
        own = [CHUNK, 0]
        gc = [[cs[j][d][top, own[d]:own[d] + 1] for d in dirs] for j in range(n)]
        grem = [[cs[j][d][bot, own[d]:own[d] + 1] for d in dirs] for j in range(n)]
        decay = [jnp.exp(jnp.where(lo, cs[j][0][top], cs[j][1][top])) for j in range(n)]
        egc = [[jnp.exp(gc[j][d]) for d in dirs] for j in range(n)]
        kb = [[k[j][d] * bg[j][d][0] for d in dirs] for j in range(n)]
        kq = []
        for j in range(n):
            kcat = jnp.concatenate([k[j][0], k[j][1]], axis=0).astype(BF16)
            kq.append([lax.dot_general(jnp.concatenate([kb[j][d], q[j][d]], axis=0).astype(BF16), kcat,
                                       nt, preferred_element_type=F32) for d in dirs])
        yield
        a = [jnp.where(strict_p, jnp.where(lo, kq[j][0][top], kq[j][1][top]) * decay[j], 0.0)
             for j in range(n)]
        for j in range(n):
            qk = jnp.where(incl_p, jnp.where(lo, kq[j][0][bot], kq[j][1][bot]) * decay[j], 0.0)
            qk_sc[buf, pl.ds(j * CHUNK, CHUNK), :] = qk.astype(BF16)
        x = [eye_p - a[j] for j in range(n)]
        p = a
        pbd = [block_diag(p[j]) for j in range(n)]
        for _ in range(5):
            p = [jnp.dot(p[j].astype(BF16), pbd[j], preferred_element_type=F32) for j in range(n)]
            yield
            pbd = [block_diag(p[j]) for j in range(n)]
            x = [x[j] + jnp.dot(x[j].astype(BF16), pbd[j], preferred_element_type=F32) for j in range(n)]
            yield
        uw = []
        for j in range(n):
            r_f = jnp.concatenate([v[j][0] * bg[j][0][0], kb[j][0] * egc[j][0], zeros_c, zeros_c], axis=1)
            r_b = jnp.concatenate([zeros_c, zeros_c, v[j][1] * bg[j][1][0], kb[j][1] * egc[j][1]], axis=1)
            uw.append(jnp.dot(x[j].astype(BF16), jnp.concatenate([r_f, r_b], axis=0).astype(BF16),
                              preferred_element_type=F32))
        yield
        for j in range(n):
            for d in dirs:
                u = uw[j][:, 2 * d * LANES:(2 * d + 1) * LANES]
                w = uw[j][:, (2 * d + 1) * LANES:(2 * d + 2) * LANES]
                kd = (k[j][d] * jnp.exp(grem[j][d])).astype(BF16)
                w2b = lax.dot_general(kd, jnp.concatenate([w, u], axis=1).astype(BF16), tn,
                                      preferred_element_type=F32)
                blk = pl.ds(j * 2 * CHUNK, 2 * CHUNK)
                bd = buf * 2 + d
                u_sc[bd, pl.ds(j * CHUNK, CHUNK), :] = u
                wq_sc[bd, blk, :] = jnp.concatenate([w, q[j][d] * egc[j][d]], axis=0).astype(BF16)
                w2_sc[bd, blk, :] = w2b[:, :LANES].astype(BF16)
                b_sc[bd, blk, :] = w2b[:, LANES:]
                tot = gc[j][d][CHUNK - 1:CHUNK, :] if d == 0 else gc[j][d][0:1, :]
                gt_sc[bd, pl.ds(j * 8, 8), :] = jnp.broadcast_to(jnp.exp(tot), (8, LANES))
        yield

    def recur_group(buf, toks):
        s = [s_sc[0], s_sc[1]]
        for j in range(_GROUP):
            blk = pl.ds(j * 2 * CHUNK, 2 * CHUNK)
            ws, sw = [], []
            for d in dirs:
                sb = s[d].astype(BF16)
                ws.append(jnp.dot(w2_sc[buf * 2 + d, blk, :], sb, preferred_element_type=F32))
                sw.append(jnp.dot(wq_sc[buf * 2 + d, blk, :], sb, preferred_element_type=F32))
                yield
            s = [s[d] * gt_sc[buf * 2 + d, pl.ds(j * 8, 1), :] + b_sc[buf * 2 + d, blk, :] - ws[d] for d in dirs]
            vn = [u_sc[buf * 2 + d, pl.ds(j * CHUNK, CHUNK), :] - sw[d][top] for d in dirs]
            vbd = jnp.concatenate([jnp.concatenate([vn[0], zeros_c], axis=1),
                                   jnp.concatenate([zeros_c, vn[1]], axis=1)], axis=0).astype(BF16)
            oo = jnp.dot(qk_sc[buf, pl.ds(j * CHUNK, CHUNK), :], vbd, preferred_element_type=F32)
            of_sc[pl.ds(toks[j][0], CHUNK), :] = sw[0][bot] + oo[:, :LANES]
            ob_sc[pl.ds(toks[j][1], CHUNK), :] = sw[1][bot] + oo[:, LANES:]
            yield
        s_sc[0] = s[0]
        s_sc[1] = s[1]

    n_stage = 14
    n_piece = 3 * _GROUP

    for _ in prep_group(group_toks(0), 0):
        pass
    s_sc[...] = jnp.zeros(s_sc.shape, F32)

    def group_body(gi, carry):
        buf = gi % 2
        nxt = prep_group(group_toks(gi + 1), 1 - buf)
        cur = recur_group(buf, group_toks(gi))
        done = 0
        for st in range(n_stage):
            next(nxt)
            want = -(-n_piece * (st + 1) // n_stage)
            for _ in range(want - done):
                next(cur)
            done = want
        for _ in cur:
            pass
        return carry

    lax.fori_loop(0, n_groups - 1, group_body, 0)
    for _ in recur_group((n_groups - 1) % 2, group_toks(n_groups - 1)):
        pass

    blk = 256

    def fin(i, carry):
        s = pl.multiple_of(i * blk, blk)
        o = of_sc[pl.ds(s, blk), :] + ob_sc[pl.ds(s, blk), :]
        o = o * lax.rsqrt(jnp.mean(o * o, axis=-1, keepdims=True) + NORM_EPS) * nw_ref[...]
        z = z_ref[0, pl.ds(s, blk), :].astype(F32)
        o_ref[0, pl.ds(s, blk), :] = (o * (z / (1.0 + jnp.exp(-z)))).astype(BF16)
        return carry
    lax.fori_loop(0, T // blk, fin, 0)


def _deltanet(dn, z, ba, conv_w, A_log, dt_bias, dn_norm_w):
    B, T, _ = dn.shape
    H = N_DN_HEADS
    ba_t = ba.reshape(B, T, 2, 2, H).transpose(0, 4, 1, 2, 3).reshape(B, H, T, 4)
    cw = jnp.pad(conv_w, ((0, 8 - CONV_W), (0, 0))).reshape(8, 3, H, DN_HEAD_DIM).transpose(2, 1, 0, 3)
    cw = cw.reshape(H * 3, 8, DN_HEAD_DIM)
    ad = jnp.concatenate([A_log, dt_bias], axis=0).T
    ad = jnp.broadcast_to(ad[:, :, None], (H, 4, LANES))
    ad = jnp.pad(ad, ((0, 0), (0, 4), (0, 0)))
    gsz = _GROUP * CHUNK
    kern = functools.partial(_dn_kernel, T=T)
    col = lambda off: pl.BlockSpec((1, T, DN_HEAD_DIM), lambda b, h, off=off: (b, 0, off + h))
    return pl.pallas_call(
        kern,
        grid=(B, H),
        in_specs=[col(0), col(H), col(2 * H),
                  pl.BlockSpec((1, T, DN_HEAD_DIM), lambda b, h: (b, 0, h)),
                  pl.BlockSpec((1, 1, T, 4), lambda b, h: (b, h, 0, 0)),
                  pl.BlockSpec((3, 8, DN_HEAD_DIM), lambda b, h: (h, 0, 0)),
                  pl.BlockSpec((1, 8, LANES), lambda b, h: (h, 0, 0)),
                  pl.BlockSpec((1, DN_HEAD_DIM), lambda b, h: (0, 0))],
        out_specs=pl.BlockSpec((1, T, DN_HEAD_DIM), lambda b, h: (b, 0, h)),
        out_shape=jax.ShapeDtypeStruct((B, T, DN_W), BF16),
        scratch_shapes=[pltpu.VMEM((T + 16, LANES), F32),
                        pltpu.VMEM((T, LANES), F32),
                        pltpu.VMEM((T, LANES), F32),
                        pltpu.VMEM((T, LANES), F32),
                        pltpu.VMEM((4, gsz, LANES), F32),
                        pltpu.VMEM((4, 2 * gsz, LANES), BF16),
                        pltpu.VMEM((4, 2 * gsz, LANES), BF16),
                        pltpu.VMEM((4, 2 * gsz, LANES), F32),
                        pltpu.VMEM((2, gsz, LANES), BF16),
                        pltpu.VMEM((4, _GROUP * 8, LANES), F32),
                        pltpu.VMEM((2, DN_HEAD_DIM, DN_HEAD_DIM), F32),
                        pltpu.VMEM((T, LANES), F32),
                        pltpu.VMEM((T, LANES), F32)],
        compiler_params=_cparams(("parallel", "parallel")),
        name="deltanet",
    )(dn, dn, dn, z, ba_t, cw, ad, dn_norm_w[None, :])


def _outproj_kernel(x_ref, at_ref, dn_ref, wo_ref, n2w_ref, rw_ref, rb_ref,
                    h_ref, h2_ref, idx_ref, gate_ref):
    mix = (jnp.dot(at_ref[...], wo_ref[0:ATTN_W, :], preferred_element_type=F32)
           + jnp.dot(dn_ref[...], wo_ref[ATTN_W:, :], preferred_element_type=F32))
    hres = x_ref[...] + mix
    h_ref[...] = hres
    h2 = hres * lax.rsqrt(jnp.mean(hres * hres, axis=-1, keepdims=True) + NORM_EPS) * n2w_ref[...]
    h2_ref[...] = h2
    logits = jnp.dot(h2, rw_ref[...], preferred_element_type=F32,
                     precision=lax.Precision.HIGHEST) + rb_ref[...]
    lane = lax.broadcasted_iota(jnp.int32, logits.shape, 1).astype(F32)
    vals = []
    idxs = []
    l = logits
    for _ in range(TOP_K):
        m = jnp.max(l, axis=-1, keepdims=True)
        i = jnp.min(jnp.where(l == m, lane, float(LANES)), axis=-1, keepdims=True)
        vals.append(m)
        idxs.append(i)
        l = jnp.where(lane == i, -jnp.inf, l)
    es = [jnp.exp(v - vals[0]) for v in vals]
    den = es[0] + es[1] + es[2] + es[3]
    idx_ref[...] = jnp.concatenate(idxs, axis=1).astype(jnp.int32)
    gate_ref[...] = jnp.concatenate([e / den for e in es], axis=1)


def _outproj_router(x2d, attn2d, dn2d, w_out_b, norm2_w, router_w, router_b, tm=512):
    n = x2d.shape[0]
    rw = jnp.pad(router_w, ((0, 0), (0, LANES - N_EXPERTS)))
    rb = jnp.pad(router_b, (0, LANES - N_EXPERTS), constant_values=-jnp.inf)[None, :]
    const = lambda shape: pl.BlockSpec(shape, lambda i: (0,) * len(shape))
    row = lambda w: pl.BlockSpec((tm, w), lambda i: (i, 0))
    return pl.pallas_call(
        _outproj_kernel,
        grid=(n // tm,),
        in_specs=[row(D_MODEL), row(ATTN_W), row(DN_W), const((D_MODEL, D_MODEL)),
                  const((1, D_MODEL)), const((D_MODEL, LANES)), const((1, LANES))],
        out_specs=[row(D_MODEL), row(D_MODEL), row(TOP_K), row(TOP_K)],
        out_shape=[jax.ShapeDtypeStruct((n, D_MODEL), F32),
                   jax.ShapeDtypeStruct((n, D_MODEL), F32),
                   jax.ShapeDtypeStruct((n, TOP_K), jnp.int32),
                   jax.ShapeDtypeStruct((n, TOP_K), F32)],
        compiler_params=_cparams(("parallel",)),
        name="outproj_router",
    )(x2d, attn2d, dn2d, w_out_b, norm2_w[None, :], rw, rb)


_BM = 256


def _moe_kernel(be_ref, nu_ref, tok_ref, tokn_ref, dstp_ref, dstc_ref, h2_hbm, w1_ref, b1_ref, w2_ref,
                b2_ref, yk_hbm, xbuf, ybuf, act_sc, gsem, ssem, *, n_assign):
    i = pl.program_id(0)
    n_used = nu_ref[0]
    slot = i % 2
    other = 1 - slot
    nc = 256

    def gather_row(idx_ref, sl, rr):
        return pltpu.make_async_copy(h2_hbm.at[pl.ds(idx_ref[0, 0, rr], 1)], xbuf.at[sl, pl.ds(rr, 1)],
                                     gsem.at[sl])

    def scatter_row(dst_ref, sl, rr):
        return pltpu.make_async_copy(ybuf.at[sl, pl.ds(rr, 1)], yk_hbm.at[pl.ds(dst_ref[0, 0, rr], 1)],
                                     ssem.at[sl])

    def wait_gather(sl):
        pltpu.make_async_copy(h2_hbm.at[pl.ds(0, _BM)], xbuf.at[sl], gsem.at[sl]).wait()

    def wait_scatter(sl):
        pltpu.make_async_copy(ybuf.at[sl], yk_hbm.at[pl.ds(0, _BM)], ssem.at[sl]).wait()

    @pl.when(i == 0)
    def _():
        ybuf[...] = jnp.zeros(ybuf.shape, F32)
        pltpu.make_async_copy(ybuf.at[0], yk_hbm.at[pl.ds(n_assign, _BM)], ssem.at[0]).start()

        def body(rr, c):
            gather_row(tok_ref, 0, rr).start()
            return c
        lax.fori_loop(0, _BM, body, 0, unroll=8)

    @pl.when(i < n_used)
    def _():
        wait_gather(slot)
        jobs = ([functools.partial(gather_row, tokn_ref, other, rr) for rr in range(_BM)]
                + [functools.partial(scatter_row, dstp_ref, other, rr) for rr in range(_BM)])
        n_gaps = D_FF // nc + D_MODEL // nc
        per_gap = -(-len(jobs) // n_gaps)

        def issue(gap):
            for job in jobs[gap * per_gap:(gap + 1) * per_gap]:
                job().start()

        x = xbuf[slot].astype(BF16)
        for j in range(D_FF // nc):
            cg = slice(j * nc, (j + 1) * nc)
            cu = slice(D_FF + j * nc, D_FF + (j + 1) * nc)
            gate = jnp.dot(x, w1_ref[0, :, cg], preferred_element_type=F32) + b1_ref[0, :, cg]
            up = jnp.dot(x, w1_ref[0, :, cu], preferred_element_type=F32) + b1_ref[0, :, cu]
            gate = jnp.minimum(gate, SWIGLU_LIMIT)
            up = jnp.clip(up, -SWIGLU_LIMIT, SWIGLU_LIMIT)
            act = gate * (1.0 / (1.0 + jnp.exp(-SWIGLU_ALPHA * gate))) * (up + 1.0)
            act_sc[:, cg] = act.astype(BF16)
            issue(j)
        wait_scatter(slot)
        a = act_sc[...]
        for j in range(D_MODEL // nc):
            cn = slice(j * nc, (j + 1) * nc)
            ybuf[slot, :, cn] = jnp.dot(a, w2_ref[0, :, cn], preferred_element_type=F32) + b2_ref[0, :, cn]
            issue(D_FF // nc + j)

        @pl.when(i == n_used - 1)
        def _():
            def body(rr, c):
                scatter_row(dstc_ref, slot, rr).start()
                return c
            lax.fori_loop(0, _BM, body, 0, unroll=8)
            wait_scatter(other)
            wait_scatter(slot)
            wait_gather(other)


def _moe_blocks(n_tok):
    n_assign = n_tok * TOP_K
    return n_assign // _BM + N_EXPERTS


def _routing_plan(idx):
    n_tok = idx.shape[0]
    n_assign = n_tok * TOP_K
    nb = _moe_blocks(n_tok)
    flat_e = idx.reshape(-1)
    order = jnp.argsort(flat_e, stable=True).astype(jnp.int32)
    counts = jnp.sum(flat_e[:, None] == jnp.arange(N_EXPERTS, dtype=jnp.int32)[None, :], axis=0,
                     dtype=jnp.int32)
    padded = (counts + _BM - 1) // _BM * _BM
    start = jnp.cumsum(counts) - counts
    pad_end = jnp.cumsum(padded)
    pad_start = pad_end - padded
    n_used = (pad_end[-1] // _BM).astype(jnp.int32)
    bidx = jnp.arange(nb, dtype=jnp.int32)
    blk = jnp.minimum(bidx, n_used - 1) * _BM
    block_e = jnp.minimum(jnp.sum(pad_end[None, :] <= blk[:, None], axis=1), N_EXPERTS - 1).astype(jnp.int32)
    lane = jnp.arange(_BM, dtype=jnp.int32)[None, :]
    rank = blk[:, None] - pad_start[block_e][:, None] + lane
    valid = rank < counts[block_e][:, None]
    src = jnp.clip(start[block_e][:, None] + rank, 0, n_assign - 1)
    assign = jnp.where(valid, order[src], 0)
    row_tok = assign // TOP_K
    dump = n_assign + (bidx[:, None] % 2) * _BM + lane
    row_dst = jnp.where(valid, (assign % TOP_K) * n_tok + row_tok, dump)
    row_dst = jnp.concatenate([n_assign + _BM + lane, row_dst], axis=0)
    return (block_e, n_used.reshape(1), row_tok.reshape(nb, 1, _BM), row_dst.reshape(nb + 1, 1, _BM))


def _moe(h2, idx, w1_b, b1, w2_b, b2):
    n_tok = h2.shape[0]
    n_assign = n_tok * TOP_K
    nb = _moe_blocks(n_tok)
    block_e, n_used, row_tok, row_dst = _routing_plan(idx)
    smem_blk = lambda f: pl.BlockSpec((1, 1, _BM), f, memory_space=pltpu.SMEM)
    grid_spec = pltpu.PrefetchScalarGridSpec(
        num_scalar_prefetch=2,
        grid=(nb,),
        in_specs=[smem_blk(lambda i, be, nu: (i, 0, 0)),
                  smem_blk(lambda i, be, nu: (jnp.minimum(i + 1, nb - 1), 0, 0)),
                  smem_blk(lambda i, be, nu: (i, 0, 0)),
                  smem_blk(lambda i, be, nu: (i + 1, 0, 0)),
                  pl.BlockSpec(memory_space=pl.ANY),
                  pl.BlockSpec((1, D_MODEL, 2 * D_FF), lambda i, be, nu: (be[i], 0, 0)),
                  pl.BlockSpec((1, 1, 2 * D_FF), lambda i, be, nu: (be[i], 0, 0)),
                  pl.BlockSpec((1, D_FF, D_MODEL), lambda i, be, nu: (be[i], 0, 0)),
                  pl.BlockSpec((1, 1, D_MODEL), lambda i, be, nu: (be[i], 0, 0))],
        out_specs=pl.BlockSpec(memory_space=pl.ANY),
        scratch_shapes=[pltpu.VMEM((2, _BM, D_MODEL), F32),
                        pltpu.VMEM((2, _BM, D_MODEL), F32),
                        pltpu.VMEM((_BM, D_FF), BF16),
                        pltpu.SemaphoreType.DMA((2,)),
                        pltpu.SemaphoreType.DMA((2,))],
    )
    return pl.pallas_call(
        functools.partial(_moe_kernel, n_assign=n_assign),
        grid_spec=grid_spec,
        out_shape=jax.ShapeDtypeStruct((n_assign + 2 * _BM, D_MODEL), F32),
        compiler_params=_cparams(("arbitrary",)),
        name="moe_experts",
    )(block_e, n_used, row_tok, row_tok, row_dst, row_dst, h2, w1_b, b1[:, None, :], w2_b, b2[:, None, :])


def _combine_kernel(h_ref, y0_ref, y1_ref, y2_ref, y3_ref, g_ref, o_ref):
    g = g_ref[...]
    acc = h_ref[...]
    for k, y_ref in enumerate((y0_ref, y1_ref, y2_ref, y3_ref)):
        acc = acc + g[:, k:k + 1] * y_ref[...]
    o_ref[...] = acc


def _combine(h, yk, gates, tm=256):
    n = h.shape[0]
    nt = n // tm
    row = lambda w: pl.BlockSpec((tm, w), lambda i: (i, 0))
    ysp = lambda k: pl.BlockSpec((tm, D_MODEL), lambda i, k=k: (k * nt + i, 0))
    return pl.pallas_call(
        _combine_kernel,
        grid=(nt,),
        in_specs=[row(D_MODEL), ysp(0), ysp(1), ysp(2), ysp(3), row(TOP_K)],
        out_specs=row(D_MODEL),
        out_shape=jax.ShapeDtypeStruct((n, D_MODEL), F32),
        compiler_params=_cparams(("parallel",)),
        name="moe_combine",
    )(h, yk, yk, yk, yk, gates)


def _layer(x, p):
    B, T, D = x.shape
    x2d = x.reshape(B * T, D)
    q, k, v, dn, z, ba = _inproj(x2d, T, p["norm1_w"], p["w_in"], p["q_norm_w"], p["k_norm_w"])
    attn = _attention(q.reshape(B, T, -1), k.reshape(B, T, -1), v.reshape(B, T, -1))
    dno = _deltanet(dn.reshape(B, T, -1), z.reshape(B, T, -1), ba.reshape(B, T, -1),
                    p["conv_w"], p["A_log"], p["dt_bias"], p["dn_norm_w"])
    h, h2, idx, gates = _outproj_router(x2d, attn.reshape(B * T, -1), dno.reshape(B * T, -1),
                                        p["w_out"], p["norm2_w"], p["router_w"], p["router_b"])
    yk = _moe(h2, idx, p["w1"], p["b1"], p["w2"], p["b2"])
    y = _combine(h, yk, gates)
    return y.reshape(B, T, D)


def kernel(x_prompt, x_sample, norm1_w, w_in, conv_w, q_norm_w, k_norm_w, A_log, dt_bias, dn_norm_w,
           w_out, norm2_w, router_w, router_b, w1, b1, w2, b2):
    l = 0
    p = {
        "norm1_w": norm1_w[l],
        "w_in": jnp.pad(w_in[l], ((0, 0), (0, PROJ_PAD - w_in.shape[-1]))).astype(BF16),
        "conv_w": conv_w[l], "q_norm_w": q_norm_w[l], "k_norm_w": k_norm_w[l],
        "A_log": A_log[l], "dt_bias": dt_bias[l], "dn_norm_w": dn_norm_w[l],
        "w_out": w_out[l].astype(BF16), "norm2_w": norm2_w[l],
        "router_w": router_w[l], "router_b": router_b[l],
        "w1": w1[l].astype(BF16), "b1": b1[l], "w2": w2[l].astype(BF16), "b2": b2[l],
    }
    return (_layer(x_prompt, p), _layer(x_sample, p))
```

```python
import functools
import math

import numpy as np
import jax
import jax.numpy as jnp
from jax import lax
from jax.experimental import pallas as pl
from jax.experimental.pallas import tpu as pltpu

F32 = jnp.float32
BF16 = jnp.bfloat16

D_MODEL = 1024
ATTN_W = 512
HEAD_DIM = 64
N_Q_HEADS = 8
N_KV_HEADS = 2
KV_W = 128
DN_W = 512
DN_HEAD_DIM = 128
N_DN_HEADS = 4
CONV_W = 5
CHUNK = 64
GRID_W = 64
ROPE_THETA = 10000.0
N_EXPERTS = 32
TOP_K = 4
D_FF = 1024
SWIGLU_LIMIT = 7.0
SWIGLU_ALPHA = 1.702
NORM_EPS = 1e-6

LANES = 128
PROJ_PAD = 2944
VMEM_LIMIT = 56 * 1024 * 1024

_C_Q = 0
_C_KV = ATTN_W
_C_DN = _C_KV + 2 * KV_W
_C_Z = _C_DN + 3 * DN_W
_C_BA = _C_Z + DN_W


def _cparams(sem):
    return pltpu.CompilerParams(dimension_semantics=sem, vmem_limit_bytes=VMEM_LIMIT)


def _rope_tables(T):
    t = np.arange(T)
    row = (t // GRID_W).astype(np.float64)
    col = (t % GRID_W).astype(np.float64)
    half = HEAD_DIM // 2
    freqs = ROPE_THETA ** (-np.arange(0, half, 2, dtype=np.float64) / half)
    lane = np.arange(LANES)
    d = lane % HEAD_DIM
    use_col = (d // half) == 1
    f = d % (half // 2)
    first = (d % half) < (half // 2)
    pos = np.where(use_col[None, :], col[:, None], row[:, None])
    ang = pos * freqs[f][None, :]
    cos = np.cos(ang)
    sin = np.sin(ang)
    sin_a = np.where(first[None, :], -sin, 0.0)
    sin_b = np.where(first[None, :], 0.0, sin)
    return (jnp.asarray(cos, F32), jnp.asarray(sin_a, F32), jnp.asarray(sin_b, F32))


def _head_mean_matrix():
    lane = np.arange(LANES)
    m = (lane[:, None] // HEAD_DIM == lane[None, :] // HEAD_DIM).astype(np.float32) / HEAD_DIM
    return jnp.asarray(m, BF16)


def _split_bf16(x):
    hi = x.astype(BF16)
    lo = (x - hi.astype(F32)).astype(BF16)
    return hi, lo


def _inproj_kernel(x_ref, n1w_ref, w_ref, qnw_ref, knw_ref, cos_ref, sa_ref, sb_ref, hm_ref,
                   q_ref, k_ref, v_ref, dn_ref, z_ref, ba_ref):
    x = x_ref[...]
    ms = jnp.mean(x * x, axis=-1, keepdims=True)
    hn = (x * lax.rsqrt(ms + NORM_EPS) * n1w_ref[...]).astype(BF16)

    def proj(a, b):
        return jnp.dot(hn, w_ref[:, a:b], preferred_element_type=F32)

    hm = hm_ref[...]
    cos = cos_ref[...]
    sa = sa_ref[...]
    sb = sb_ref[...]
    lane = lax.broadcasted_iota(jnp.int32, (1, LANES), 1)
    lo_half = lane < HEAD_DIM

    def norm_rope(xs, w):
        hi, lo = _split_bf16(xs * xs)
        msq = (jnp.dot(hi, hm, preferred_element_type=F32)
               + jnp.dot(lo, hm, preferred_element_type=F32))
        xn = xs * lax.rsqrt(msq + NORM_EPS) * w
        return (xn * cos + pltpu.roll(xn, LANES - 16, 1) * sa + pltpu.roll(xn, 16, 1) * sb)

    q = proj(_C_Q, _C_Q + ATTN_W)
    scale = HEAD_DIM ** -0.5 * math.log2(math.e)
    for j in range(ATTN_W // LANES):
        qr = norm_rope(q[:, j * LANES:(j + 1) * LANES], qnw_ref[...] * scale)
        qs = pltpu.roll(qr, HEAD_DIM, 1)
        g = (2 * j) // (N_Q_HEADS // N_KV_HEADS)
        keep = lo_half if g == 0 else jnp.logical_not(lo_half)
        h0 = qr if g == 0 else qs
        h1 = qs if g == 0 else qr
        q_ref[:, (2 * j) * LANES:(2 * j + 1) * LANES] = jnp.where(keep, h0, 0.0).astype(BF16)
        q_ref[:, (2 * j + 1) * LANES:(2 * j + 2) * LANES] = jnp.where(keep, h1, 0.0).astype(BF16)

    kv = proj(_C_KV, _C_KV + 2 * KV_W)
    k_ref[...] = norm_rope(kv[:, :KV_W], knw_ref[...]).astype(BF16)
    v_ref[...] = kv[:, KV_W:].astype(BF16)
    dn_ref[...] = proj(_C_DN, _C_DN + 3 * DN_W).astype(BF16)
    z_ref[...] = proj(_C_Z, _C_Z + DN_W).astype(BF16)
    ba_ref[...] = proj(_C_BA, _C_BA + LANES)[:, :4 * N_DN_HEADS]


def _inproj(x2d, T, norm1_w, w_in_b, q_norm_w, k_norm_w, tm=512):
    n = x2d.shape[0]
    cos, sa, sb = _rope_tables(T)
    tm = min(tm, T)
    tpb = T // tm
    tab_spec = pl.BlockSpec((tm, LANES), lambda i: (i % tpb, 0))
    const = lambda shape: pl.BlockSpec(shape, lambda i: (0,) * len(shape))
    row = lambda w: pl.BlockSpec((tm, w), lambda i: (i, 0))
    qnw = jnp.tile(q_norm_w, LANES // HEAD_DIM)[None, :]
    knw = jnp.tile(k_norm_w, LANES // HEAD_DIM)[None, :]
    return pl.pallas_call(
        _inproj_kernel,
        grid=(n // tm,),
        in_specs=[row(D_MODEL), const((1, D_MODEL)), const((D_MODEL, PROJ_PAD)),
                  const((1, LANES)), const((1, LANES)), tab_spec, tab_spec, tab_spec,
                  const((LANES, LANES))],
        out_specs=[row(N_Q_HEADS * LANES), row(KV_W), row(KV_W), row(3 * DN_W), row(DN_W),
                   row(4 * N_DN_HEADS)],
        out_shape=[jax.ShapeDtypeStruct((n, N_Q_HEADS * LANES), BF16),
                   jax.ShapeDtypeStruct((n, KV_W), BF16),
                   jax.ShapeDtypeStruct((n, KV_W), BF16),
                   jax.ShapeDtypeStruct((n, 3 * DN_W), BF16),
                   jax.ShapeDtypeStruct((n, DN_W), BF16),
                   jax.ShapeDtypeStruct((n, 4 * N_DN_HEADS), F32)],
        compiler_params=_cparams(("parallel",)),
        name="inproj",
    )(x2d, norm1_w[None, :], w_in_b, qnw, knw, cos, sa, sb, _head_mean_matrix())


_ATTN_PARTS = 4


def _attn_kernel(q_ref, k_ref, v_ref, o_ref, m_sc, acc_sc, *, tq, tk, n_kc):
    hp = N_Q_HEADS // _ATTN_PARTS
    rows = hp * tq
    qs = [jnp.concatenate([q_ref[0, :, h * LANES:(h + 1) * LANES] for h in range(u * hp, (u + 1) * hp)],
                          axis=0) for u in range(_ATTN_PARTS)]
    m_sc[...] = jnp.full(m_sc.shape, -jnp.inf, F32)
    acc_sc[...] = jnp.zeros(acc_sc.shape, F32)
    ones = jnp.ones((tk, LANES), BF16)

    def scores(start, u):
        kc = k_ref[0, pl.ds(start, tk), :]
        return lax.dot_general(qs[u], kc, (((1,), (1,)), ((), ())), preferred_element_type=F32)

    def softmax_pv(s, start, u):
        r = slice(u * rows, (u + 1) * rows)
        m_prev = m_sc[r, :]
        m_next = jnp.maximum(m_prev, jnp.max(s, axis=1, keepdims=True))
        alpha = jnp.exp2(m_prev - m_next)
        p = jnp.concatenate(
            [jnp.exp2(s[:, j * LANES:(j + 1) * LANES] - m_next) for j in range(tk // LANES)], axis=1)
        v_aug = jnp.concatenate([v_ref[0, pl.ds(start, tk), :], ones], axis=1)
        pv = jnp.dot(p.astype(BF16), v_aug, preferred_element_type=F32)
        acc_sc[r, :] = acc_sc[r, :] * jnp.concatenate([alpha, alpha], axis=1) + pv
        m_sc[r, :] = m_next

    unroll = 4 if n_kc % 4 == 0 else (2 if n_kc % 2 == 0 else 1)

    def body(c, carry):
        units = [(pl.multiple_of((c * unroll + cc) * tk, tk), u)
                 for cc in range(unroll) for u in range(_ATTN_PARTS)]
        s = {i: scores(*units[i]) for i in range(min(2, len(units)))}
        for i, (start, u) in enumerate(units):
            softmax_pv(s.pop(i), start, u)
            if i + 2 < len(units):
                s[i + 2] = scores(*units[i + 2])
        return carry

    lax.fori_loop(0, n_kc // unroll, body, 0)

    lane = lax.broadcasted_iota(jnp.int32, (1, LANES), 1)
    lo_half = lane < HEAD_DIM
    outs = []
    for h in range(N_Q_HEADS):
        a = acc_sc[h * tq:(h + 1) * tq, :]
        outs.append(a[:, :LANES] / a[:, LANES:])
    for j in range(N_Q_HEADS // 2):
        g = (2 * j) // (N_Q_HEADS // N_KV_HEADS)
        a, b = outs[2 * j], outs[2 * j + 1]
        if g == 0:
            b = pltpu.roll(b, HEAD_DIM, 1)
        else:
            a = pltpu.roll(a, HEAD_DIM, 1)
        o_ref[0, :, j * LANES:(j + 1) * LANES] = jnp.where(lo_half, a, b).astype(BF16)


def _attention(q, k, v, tq=128, tk=512):
    B, T, _ = q.shape
    tk = min(tk, T)
    kern = functools.partial(_attn_kernel, tq=tq, tk=tk, n_kc=T // tk)
    return pl.pallas_call(
        kern,
        grid=(B, T // tq),
        in_specs=[pl.BlockSpec((1, tq, N_Q_HEADS * LANES), lambda b, i: (b, i, 0)),
                  pl.BlockSpec((1, T, KV_W), lambda b, i: (b, 0, 0)),
                  pl.BlockSpec((1, T, KV_W), lambda b, i: (b, 0, 0))],
        out_specs=pl.BlockSpec((1, tq, ATTN_W), lambda b, i: (b, i, 0)),
        out_shape=jax.ShapeDtypeStruct((B, T, ATTN_W), BF16),
        scratch_shapes=[pltpu.VMEM((N_Q_HEADS * tq, LANES), F32),
                        pltpu.VMEM((N_Q_HEADS * tq, 2 * LANES), F32)],
        compiler_params=_cparams(("parallel", "parallel")),
        name="attention",
    )(q, k, v)


_GROUP = 8


def _dn_masks():
    r = lax.broadcasted_iota(jnp.int32, (CHUNK, CHUNK), 0)
    c = lax.broadcasted_iota(jnp.int32, (CHUNK, CHUNK), 1)
    return r, c


def _dn_kernel(xq_ref, xk_ref, xv_ref, z_ref, ba_ref, cw_ref, ad_ref, nw_ref, o_ref,
               xpad, qn_sc, kn_sc, vn_sc, u_sc, wq_sc, w2_sc, b_sc, qk_sc, gt_sc, s_sc, of_sc, ob_sc,
               *, T):
    n_chunks = T // CHUNK
    n_groups = n_chunks // _GROUP
    gsz = _GROUP * CHUNK
    h = pl.program_id(1)

    def conv_pass(x_ref, w_idx, dst, mode):
        xpad[0:8, :] = jnp.zeros((8, LANES), F32)
        xpad[T + 8:T + 16, :] = jnp.zeros((8, LANES), F32)
        blk = 256

        def cp(i, c):
            s = pl.multiple_of(i * blk, blk)
            xpad[pl.ds(s + 8, blk), :] = x_ref[0, pl.ds(s, blk), :].astype(F32)
            return c
        lax.fori_loop(0, T // blk, cp, 0)
        w = cw_ref[w_idx]

        def cv(i, c):
            s = pl.multiple_of(i * blk, blk)
            acc = None
            for j in range(CONV_W):
                term = xpad[pl.ds(s + 8 + j - CONV_W // 2, blk), :] * w[j:j + 1, :]
                acc = term if acc is None else acc + term
            y = acc / (1.0 + jnp.exp(-acc))
            if mode == "q":
                y = y * lax.rsqrt(jnp.sum(y * y, axis=-1, keepdims=True) + NORM_EPS) * (DN_HEAD_DIM ** -0.5)
            elif mode == "k":
                y = y * lax.rsqrt(jnp.sum(y * y, axis=-1, keepdims=True) + NORM_EPS)
            dst[pl.ds(s, blk), :] = y
            return c
        lax.fori_loop(0, T // blk, cv, 0)

    conv_pass(xq_ref, 0, qn_sc, "q")
    conv_pass(xk_ref, 1, kn_sc, "k")
    conv_pass(xv_ref, 2, vn_sc, "v")

    r, c = _dn_masks()
    cum_ops = [jnp.concatenate([(c <= r).astype(BF16), (c > r).astype(BF16)], axis=0),
               jnp.concatenate([(c >= r).astype(BF16), (c < r).astype(BF16)], axis=0)]
    r2 = lax.broadcasted_iota(jnp.int32, (CHUNK, LANES), 0)
    c2 = lax.broadcasted_iota(jnp.int32, (CHUNK, LANES), 1)
    lo = c2 < CHUNK
    cc = jnp.where(lo, c2, c2 - CHUNK)
    gs_zero = [jnp.logical_and(lo, r2 <= cc), jnp.logical_and(jnp.logical_not(lo), r2 >= cc)]
    strict_p = jnp.where(lo, r2 - cc, cc - r2) > 0
    incl_p = jnp.where(lo, r2 - cc, cc - r2) >= 0
    eye_p = (cc == r2).astype(F32)
    zeros_c = jnp.zeros((CHUNK, LANES), F32)
    nt = (((1,), (1,)), ((), ()))
    tn = (((0,), (0,)), ((), ()))
    top, bot = slice(0, CHUNK), slice(CHUNK, 2 * CHUNK)
    dirs = range(2)

    def block_diag(p):
        return jnp.concatenate([jnp.where(lo, p, 0.0), jnp.where(lo, 0.0, p)], axis=0).astype(BF16)

    def gates(tok0, d):
        ba = ba_ref[0, 0, pl.ds(tok0, CHUNK), :]
        beta = 1.0 / (1.0 + jnp.exp(-ba[:, d:d + 1]))
        a_in = ba[:, 2 + d:3 + d] + ad_ref[0, 2 + d:3 + d, :][:, 0:1]
        sp = jnp.maximum(a_in, 0.0) + jnp.log(1.0 + jnp.exp(-jnp.abs(a_in)))
        return beta, -jnp.exp(ad_ref[0, d:d + 1, :][:, 0:1]) * sp

    def group_toks(gi):
        hint = (lambda x, m: x) if isinstance(gi, int) else pl.multiple_of
        base_f = hint(gi * gsz, gsz)
        base_b = hint((n_groups - 1 - gi) * gsz, gsz)
        return [(hint(base_f + j * CHUNK, CHUNK), hint(base_b + (_GROUP - 1 - j) * CHUNK, CHUNK))
                for j in range(_GROUP)]

    def prep_group(toks, buf):
        n = len(toks)
        q = [[qn_sc[pl.ds(t[d], CHUNK), :] for d in dirs] for t in toks]
        k = [[kn_sc[pl.ds(t[d], CHUNK), :] for d in dirs] for t in toks]
        v = [[vn_sc[pl.ds(t[d], CHUNK), :] for d in dirs] for t in toks]
        bg = [[gates(t[d], d) for d in dirs] for t in toks]
        cs = []
        for j in range(n):
            row = []
            for d in dirs:
                rhs = jnp.where(gs_zero[d], 0.0, jnp.broadcast_to(bg[j][d][1], (CHUNK, LANES)))
                m = jnp.dot(cum_ops[d], jnp.concatenate(_split_bf16(rhs), axis=1),
                            preferred_element_type=F32)
                row.append(m[:, :LANES] + m[:, LANES:])
            cs.append(row)
        yield
        own = [CHUNK, 0]
        gc = [[cs[j][d][top, own[d]:own[d] + 1] for d in dirs] for j in range(n)]
        grem = [[cs[j][d][bot, own[d]:own[d] + 1] for d in dirs] for j in range(n)]
        decay = [jnp.exp(jnp.where(lo, cs[j][0][top], cs[j][1][top])) for j in range(n)]
        egc = [[jnp.exp(gc[j][d]) for d in dirs] for j in range(n)]
        kb = [[k[j][d] * bg[j][d][0] for d in dirs] for j in range(n)]
        kq = []
        for j in range(n):
            kcat = jnp.concatenate([k[j][0], k[j][1]], axis=0).astype(BF16)
            kq.append([lax.dot_general(jnp.concatenate([kb[j][d], q[j][d]], axis=0).astype(BF16), kcat,
                                       nt, preferred_element_type=F32) for d in dirs])
        yield
        a = [jnp.where(strict_p, jnp.where(lo, kq[j][0][top], kq[j][1][top]) * decay[j], 0.0)
             for j in range(n)]
        for j in range(n):
            qk = jnp.where(incl_p, jnp.where(lo, kq[j][0][bot], kq[j][1][bot]) * decay[j], 0.0)
            qk_sc[buf, pl.ds(j * CHUNK, CHUNK), :] = qk.astype(BF16)
        x = [eye_p - a[j] for j in range(n)]
        p = a
        pbd = [block_diag(p[j]) for j in range(n)]
        for _ in range(5):
            p = [jnp.dot(p[j].astype(BF16), pbd[j], preferred_element_type=F32) for j in range(n)]
            yield
            pbd = [block_diag(p[j]) for j in range(n)]
            x = [x[j] + jnp.dot(x[j].astype(BF16), pbd[j], preferred_element_type=F32) for j in range(n)]
            yield
        uw = []
        for j in range(n):
            r_f = jnp.concatenate([v[j][0] * bg[j][0][0], kb[j][0] * egc[j][0], zeros_c, zeros_c], axis=1)
            r_b = jnp.concatenate([zeros_c, zeros_c, v[j][1] * bg[j][1][0], kb[j][1] * egc[j][1]], axis=1)
            uw.append(jnp.dot(x[j].astype(BF16), jnp.concatenate([r_f, r_b], axis=0).astype(BF16),
                              preferred_element_type=F32))
        yield
        for j in range(n):
            for d in dirs:
                u = uw[j][:, 2 * d * LANES:(2 * d + 1) * LANES]
                w = uw[j][:, (2 * d + 1) * LANES:(2 * d + 2) * LANES]
                kd = (k[j][d] * jnp.exp(grem[j][d])).astype(BF16)
                w2b = lax.dot_general(kd, jnp.concatenate([w, u], axis=1).astype(BF16), tn,
                                      preferred_element_type=F32)
                blk = pl.ds(j * 2 * CHUNK, 2 * CHUNK)
                bd = buf * 2 + d
                u_sc[bd, pl.ds(j * CHUNK, CHUNK), :] = u
                wq_sc[bd, blk, :] = jnp.concatenate([w, q[j][d] * egc[j][d]], axis=0).astype(BF16)
                w2_sc[bd, blk, :] = w2b[:, :LANES].astype(BF16)
                b_sc[bd, blk, :] = w2b[:, LANES:]
                tot = gc[j][d][CHUNK - 1:CHUNK, :] if d == 0 else gc[j][d][0:1, :]
                gt_sc[bd, pl.ds(j * 8, 8), :] = jnp.broadcast_to(jnp.exp(tot), (8, LANES))
        yield

    def recur_group(buf, toks):
        s = [s_sc[0], s_sc[1]]
        for j in range(_GROUP):
            blk = pl.ds(j * 2 * CHUNK, 2 * CHUNK)
            ws, sw = [], []
            for d in dirs:
                sb = s[d].astype(BF16)
                ws.append(jnp.dot(w2_sc[buf * 2 + d, blk, :], sb, preferred_element_type=F32))
                sw.append(jnp.dot(wq_sc[buf * 2 + d, blk, :], sb, preferred_element_type=F32))
                yield
            s = [s[d] * gt_sc[buf * 2 + d, pl.ds(j * 8, 1), :] + b_sc[buf * 2 + d, blk, :] - ws[d] for d in dirs]
            vn = [u_sc[buf * 2 + d, pl.ds(j * CHUNK, CHUNK), :] - sw[d][top] for d in dirs]
            vbd = jnp.concatenate([jnp.concatenate([vn[0], zeros_c], axis=1),
                                   jnp.concatenate([zeros_c, vn[1]], axis=1)], axis=0).astype(BF16)
            oo = jnp.dot(qk_sc[buf, pl.ds(j * CHUNK, CHUNK), :], vbd, preferred_element_type=F32)
            of_sc[pl.ds(toks[j][0], CHUNK), :] = sw[0][bot] + oo[:, :LANES]
            ob_sc[pl.ds(toks[j][1], CHUNK), :] = sw[1][bot] + oo[:, LANES:]
            yield
        s_sc[0] = s[0]
        s_sc[1] = s[1]

    n_stage = 14
    n_piece = 3 * _GROUP

    for _ in prep_group(group_toks(0), 0):
        pass
    s_sc[...] = jnp.zeros(s_sc.shape, F32)

    def group_body(gi, carry):
        buf = gi % 2
        nxt = prep_group(group_toks(gi + 1), 1 - buf)
        cur = recur_group(buf, group_toks(gi))
        done = 0
        for st in range(n_stage):
            next(nxt)
            want = -(-n_piece * (st + 1) // n_stage)
            for _ in range(want - done):
                next(cur)
            done = want
        for _ in cur:
            pass
        return carry

    lax.fori_loop(0, n_groups - 1, group_body, 0)
    for _ in recur_group((n_groups - 1) % 2, group_toks(n_groups - 1)):
        pass

    blk = 256

    def fin(i, carry):
        s = pl.multiple_of(i * blk, blk)
        o = of_sc[pl.ds(s, blk), :] + ob_sc[pl.ds(s, blk), :]
        o = o * lax.rsqrt(jnp.mean(o * o, axis=-1, keepdims=True) + NORM_EPS) * nw_ref[...]
        z = z_ref[0, pl.ds(s, blk), :].astype(F32)
        o_ref[0, pl.ds(s, blk), :] = (o * (z / (1.0 + jnp.exp(-z)))).astype(BF16)
        return carry
    lax.fori_loop(0, T // blk, fin, 0)


def _deltanet(dn, z, ba, conv_w, A_log, dt_bias, dn_norm_w):
    B, T, _ = dn.shape
    H = N_DN_HEADS
    ba_t = ba.reshape(B, T, 2, 2, H).transpose(0, 4, 1, 2, 3).reshape(B, H, T, 4)
    cw = jnp.pad(conv_w, ((0, 8 - CONV_W), (0, 0))).reshape(8, 3, H, DN_HEAD_DIM).transpose(2, 1, 0, 3)
    cw = cw.reshape(H * 3, 8, DN_HEAD_DIM)
    ad = jnp.concatenate([A_log, dt_bias], axis=0).T
    ad = jnp.broadcast_to(ad[:, :, None], (H, 4, LANES))
    ad = jnp.pad(ad, ((0, 0), (0, 4), (0, 0)))
    gsz = _GROUP * CHUNK
    kern = functools.partial(_dn_kernel, T=T)
    col = lambda off: pl.BlockSpec((1, T, DN_HEAD_DIM), lambda b, h, off=off: (b, 0, off + h))
    return pl.pallas_call(
        kern,
        grid=(B, H),
        in_specs=[col(0), col(H), col(2 * H),
                  pl.BlockSpec((1, T, DN_HEAD_DIM), lambda b, h: (b, 0, h)),
                  pl.BlockSpec((1, 1, T, 4), lambda b, h: (b, h, 0, 0)),
                  pl.BlockSpec((3, 8, DN_HEAD_DIM), lambda b, h: (h, 0, 0)),
                  pl.BlockSpec((1, 8, LANES), lambda b, h: (h, 0, 0)),
                  pl.BlockSpec((1, DN_HEAD_DIM), lambda b, h: (0, 0))],
        out_specs=pl.BlockSpec((1, T, DN_HEAD_DIM), lambda b, h: (b, 0, h)),
        out_shape=jax.ShapeDtypeStruct((B, T, DN_W), BF16),
        scratch_shapes=[pltpu.VMEM((T + 16, LANES), F32),
                        pltpu.VMEM((T, LANES), F32),
                        pltpu.VMEM((T, LANES), F32),
                        pltpu.VMEM((T, LANES), F32),
                        pltpu.VMEM((4, gsz, LANES), F32),
                        pltpu.VMEM((4, 2 * gsz, LANES), BF16),
                        pltpu.VMEM((4, 2 * gsz, LANES), BF16),
                        pltpu.VMEM((4, 2 * gsz, LANES), F32),
                        pltpu.VMEM((2, gsz, LANES), BF16),
                        pltpu.VMEM((4, _GROUP * 8, LANES), F32),
                        pltpu.VMEM((2, DN_HEAD_DIM, DN_HEAD_DIM), F32),
                        pltpu.VMEM((T, LANES), F32),
                        pltpu.VMEM((T, LANES), F32)],
        compiler_params=_cparams(("parallel", "parallel")),
        name="deltanet",
    )(dn, dn, dn, z, ba_t, cw, ad, dn_norm_w[None, :])


def _outproj_kernel(x_ref, at_ref, dn_ref, wo_ref, n2w_ref, rw_ref, rb_ref,
                    h_ref, h2_ref, idx_ref, gate_ref):
    mix = (jnp.dot(at_ref[...], wo_ref[0:ATTN_W, :], preferred_element_type=F32)
           + jnp.dot(dn_ref[...], wo_ref[ATTN_W:, :], preferred_element_type=F32))
    hres = x_ref[...] + mix
    h_ref[...] = hres
    h2 = hres * lax.rsqrt(jnp.mean(hres * hres, axis=-1, keepdims=True) + NORM_EPS) * n2w_ref[...]
    h2_ref[...] = h2
    h2_hi, h2_lo = _split_bf16(h2)
    hh = jnp.dot(h2_hi, rw_ref[...], preferred_element_type=F32)
    logits = (hh[:, :LANES] + hh[:, LANES:] + jnp.dot(h2_lo, rw_ref[:, :LANES], preferred_element_type=F32)
              + rb_ref[...])
    lane = lax.broadcasted_iota(jnp.int32, logits.shape, 1).astype(F32)
    vals = []
    idxs = []
    l = logits
    for _ in range(TOP_K):
        m = jnp.max(l, axis=-1, keepdims=True)
        i = jnp.min(jnp.where(l == m, lane, float(LANES)), axis=-1, keepdims=True)
        vals.append(m)
        idxs.append(i)
        l = jnp.where(lane == i, -jnp.inf, l)
    es = [jnp.exp(v - vals[0]) for v in vals]
    den = es[0] + es[1] + es[2] + es[3]
    idx_ref[...] = jnp.concatenate(idxs, axis=1).astype(jnp.int32)
    gate_ref[...] = jnp.concatenate([e / den for e in es], axis=1)


def _outproj_router(x2d, attn2d, dn2d, w_out_b, norm2_w, router_w, router_b, tm=512):
    n = x2d.shape[0]
    rw = jnp.concatenate(_split_bf16(jnp.pad(router_w, ((0, 0), (0, LANES - N_EXPERTS)))), axis=1)
    rb = jnp.pad(router_b, (0, LANES - N_EXPERTS), constant_values=-jnp.inf)[None, :]
    const = lambda shape: pl.BlockSpec(shape, lambda i: (0,) * len(shape))
    row = lambda w: pl.BlockSpec((tm, w), lambda i: (i, 0))
    return pl.pallas_call(
        _outproj_kernel,
        grid=(n // tm,),
        in_specs=[row(D_MODEL), row(ATTN_W), row(DN_W), const((D_MODEL, D_MODEL)),
                  const((1, D_MODEL)), const((D_MODEL, 2 * LANES)), const((1, LANES))],
        out_specs=[row(D_MODEL), row(D_MODEL), row(TOP_K), row(TOP_K)],
        out_shape=[jax.ShapeDtypeStruct((n, D_MODEL), F32),
                   jax.ShapeDtypeStruct((n, D_MODEL), F32),
                   jax.ShapeDtypeStruct((n, TOP_K), jnp.int32),
                   jax.ShapeDtypeStruct((n, TOP_K), F32)],
        compiler_params=_cparams(("parallel",)),
        name="outproj_router",
    )(x2d, attn2d, dn2d, w_out_b, norm2_w[None, :], rw, rb)


_BM = 256


def _moe_kernel(be_ref, nu_ref, tok_ref, tokn_ref, dstp_ref, dstc_ref, h2_hbm, w1_ref, b1_ref, w2_ref,
                b2_ref, yk_hbm, xbuf, ybuf, act_sc, gsem, ssem, *, n_assign):
    i = pl.program_id(0)
    n_used = nu_ref[0]
    nc = 256

    def gather_row(idx_ref, sl, rr):
        return pltpu.make_async_copy(h2_hbm.at[pl.ds(idx_ref[0, 0, rr], 1)], xbuf.at[sl, pl.ds(rr, 1)],
                                     gsem.at[sl])

    def scatter_row(dst_ref, sl, rr):
        return pltpu.make_async_copy(ybuf.at[sl, pl.ds(rr, 1)], yk_hbm.at[pl.ds(dst_ref[0, 0, rr], 1)],
                                     ssem.at[sl])

    def wait_gather(sl):
        pltpu.make_async_copy(h2_hbm.at[pl.ds(0, _BM)], xbuf.at[sl], gsem.at[sl]).wait()

    def wait_scatter(sl):
        pltpu.make_async_copy(ybuf.at[sl], yk_hbm.at[pl.ds(0, _BM)], ssem.at[sl]).wait()

    @pl.when(i == 0)
    def _():
        ybuf[...] = jnp.zeros(ybuf.shape, F32)
        pltpu.make_async_copy(ybuf.at[0], yk_hbm.at[pl.ds(n_assign, _BM)], ssem.at[0]).start()

        def body(rr, c):
            gather_row(tok_ref, 0, rr).start()
            return c
        lax.fori_loop(0, _BM, body, 0, unroll=8)

    def step(slot, other):
        wait_gather(slot)
        jobs = ([functools.partial(gather_row, tokn_ref, other, rr) for rr in range(_BM)]
                + [functools.partial(scatter_row, dstp_ref, other, rr) for rr in range(_BM)])
        n_gaps = D_FF // nc + D_MODEL // nc
        per_gap = -(-len(jobs) // n_gaps)

        def issue(gap):
            for job in jobs[gap * per_gap:(gap + 1) * per_gap]:
                job().start()

        x = xbuf[slot].astype(BF16)
        for j in range(D_FF // nc):
            cg = slice(j * nc, (j + 1) * nc)
            cu = slice(D_FF + j * nc, D_FF + (j + 1) * nc)
            gate = jnp.dot(x, w1_ref[0, :, cg], preferred_element_type=F32) + b1_ref[0, :, cg]
            up = jnp.dot(x, w1_ref[0, :, cu], preferred_element_type=F32) + b1_ref[0, :, cu]
            gate = jnp.minimum(gate, SWIGLU_LIMIT)
            up = jnp.clip(up, -SWIGLU_LIMIT, SWIGLU_LIMIT)
            act = gate * (1.0 / (1.0 + jnp.exp(-SWIGLU_ALPHA * gate))) * (up + 1.0)
            act_sc[:, cg] = act.astype(BF16)
            issue(j)
        wait_scatter(slot)
        a = act_sc[...]
        for j in range(D_MODEL // nc):
            cn = slice(j * nc, (j + 1) * nc)
            ybuf[slot, :, cn] = jnp.dot(a, w2_ref[0, :, cn], preferred_element_type=F32) + b2_ref[0, :, cn]
            issue(D_FF // nc + j)

        @pl.when(i == n_used - 1)
        def _():
            def body(rr, c):
                scatter_row(dstc_ref, slot, rr).start()
                return c
            lax.fori_loop(0, _BM, body, 0, unroll=8)
            wait_scatter(other)
            wait_scatter(slot)
            wait_gather(other)

    for parity in range(2):
        pl.when(jnp.logical_and(i < n_used, i % 2 == parity))(functools.partial(step, parity, 1 - parity))


def _moe_blocks(n_tok):
    n_assign = n_tok * TOP_K
    return n_assign // _BM + N_EXPERTS


def _routing_plan(idx):
    n_tok = idx.shape[0]
    n_assign = n_tok * TOP_K
    nb = _moe_blocks(n_tok)
    flat_e = idx.reshape(-1)
    order = jnp.argsort(flat_e, stable=True).astype(jnp.int32)
    counts = jnp.sum(flat_e[:, None] == jnp.arange(N_EXPERTS, dtype=jnp.int32)[None, :], axis=0,
                     dtype=jnp.int32)
    padded = (counts + _BM - 1) // _BM * _BM
    start = jnp.cumsum(counts) - counts
    pad_end = jnp.cumsum(padded)
    pad_start = pad_end - padded
    n_used = (pad_end[-1] // _BM).astype(jnp.int32)
    bidx = jnp.arange(nb, dtype=jnp.int32)
    blk = jnp.minimum(bidx, n_used - 1) * _BM
    block_e = jnp.minimum(jnp.sum(pad_end[None, :] <= blk[:, None], axis=1), N_EXPERTS - 1).astype(jnp.int32)
    lane = jnp.arange(_BM, dtype=jnp.int32)[None, :]
    rank = blk[:, None] - pad_start[block_e][:, None] + lane
    valid = rank < counts[block_e][:, None]
    src = jnp.clip(start[block_e][:, None] + rank, 0, n_assign - 1)
    assign = jnp.where(valid, order[src], 0)
    row_tok = assign // TOP_K
    dump = n_assign + (bidx[:, None] % 2) * _BM + lane
    row_dst = jnp.where(valid, (assign % TOP_K) * n_tok + row_tok, dump)
    row_dst = jnp.concatenate([n_assign + _BM + lane, row_dst], axis=0)
    return (block_e, n_used.reshape(1), row_tok.reshape(nb, 1, _BM), row_dst.reshape(nb + 1, 1, _BM))


def _moe(h2, idx, w1_b, b1, w2_b, b2):
    n_tok = h2.shape[0]
    n_assign = n_tok * TOP_K
    nb = _moe_blocks(n_tok)
    block_e, n_used, row_tok, row_dst = _routing_plan(idx)
    smem_blk = lambda f: pl.BlockSpec((1, 1, _BM), f, memory_space=pltpu.SMEM)
    grid_spec = pltpu.PrefetchScalarGridSpec(
        num_scalar_prefetch=2,
        grid=(nb,),
        in_specs=[smem_blk(lambda i, be, nu: (i, 0, 0)),
                  smem_blk(lambda i, be, nu: (jnp.minimum(i + 1, nb - 1), 0, 0)),
                  smem_blk(lambda i, be, nu: (i, 0, 0)),
                  smem_blk(lambda i, be, nu: (i + 1, 0, 0)),
                  pl.BlockSpec(memory_space=pl.ANY),
                  pl.BlockSpec((1, D_MODEL, 2 * D_FF), lambda i, be, nu: (be[i], 0, 0)),
                  pl.BlockSpec((1, 1, 2 * D_FF), lambda i, be, nu: (be[i], 0, 0)),
                  pl.BlockSpec((1, D_FF, D_MODEL), lambda i, be, nu: (be[i], 0, 0)),
                  pl.BlockSpec((1, 1, D_MODEL), lambda i, be, nu: (be[i], 0, 0))],
        out_specs=pl.BlockSpec(memory_space=pl.ANY),
        scratch_shapes=[pltpu.VMEM((2, _BM, D_MODEL), F32),
                        pltpu.VMEM((2, _BM, D_MODEL), F32),
                        pltpu.VMEM((_BM, D_FF), BF16),
                        pltpu.SemaphoreType.DMA((2,)),
                        pltpu.SemaphoreType.DMA((2,))],
    )
    return pl.pallas_call(
        functools.partial(_moe_kernel, n_assign=n_assign),
        grid_spec=grid_spec,
        out_shape=jax.ShapeDtypeStruct((n_assign + 2 * _BM, D_MODEL), F32),
        compiler_params=_cparams(("arbitrary",)),
        name="moe_experts",
    )(block_e, n_used, row_tok, row_tok, row_dst, row_dst, h2, w1_b, b1[:, None, :], w2_b, b2[:, None, :])


def _combine_kernel(h_ref, y0_ref, y1_ref, y2_ref, y3_ref, g_ref, o_ref):
    g = g_ref[...]
    acc = h_ref[...]
    for k, y_ref in enumerate((y0_ref, y1_ref, y2_ref, y3_ref)):
        acc = acc + g[:, k:k + 1] * y_ref[...]
    o_ref[...] = acc


def _combine(h, yk, gates, tm=256):
    n = h.shape[0]
    nt = n // tm
    row = lambda w: pl.BlockSpec((tm, w), lambda i: (i, 0))
    ysp = lambda k: pl.BlockSpec((tm, D_MODEL), lambda i, k=k: (k * nt + i, 0))
    return pl.pallas_call(
        _combine_kernel,
        grid=(nt,),
        in_specs=[row(D_MODEL), ysp(0), ysp(1), ysp(2), ysp(3), row(TOP_K)],
        out_specs=row(D_MODEL),
        out_shape=jax.ShapeDtypeStruct((n, D_MODEL), F32),
        compiler_params=_cparams(("parallel",)),
        name="moe_combine",
    )(h, yk, yk, yk, yk, gates)


def _layer(x, p):
    B, T, D = x.shape
    x2d = x.reshape(B * T, D)
    q, k, v, dn, z, ba = _inproj(x2d, T, p["norm1_w"], p["w_in"], p["q_norm_w"], p["k_norm_w"])
    attn = _attention(q.reshape(B, T, -1), k.reshape(B, T, -1), v.reshape(B, T, -1))
    dno = _deltanet(dn.reshape(B, T, -1), z.reshape(B, T, -1), ba.reshape(B, T, -1),
                    p["conv_w"], p["A_log"], p["dt_bias"], p["dn_norm_w"])
    h, h2, idx, gates = _outproj_router(x2d, attn.reshape(B * T, -1), dno.reshape(B * T, -1),
                                        p["w_out"], p["norm2_w"], p["router_w"], p["router_b"])
    yk = _moe(h2, idx, p["w1"], p["b1"], p["w2"], p["b2"])
    y = _combine(h, yk, gates)
    return y.reshape(B, T, D)


def kernel(x_prompt, x_sample, norm1_w, w_in, conv_w, q_norm_w, k_norm_w, A_log, dt_bias, dn_norm_w,
           w_out, norm2_w, router_w, router_b, w1, b1, w2, b2):
    l = 0
    p = {
        "norm1_w": norm1_w[l],
        "w_in": jnp.pad(w_in[l], ((0, 0), (0, PROJ_PAD - w_in.shape[-1]))).astype(BF16),
        "conv_w": conv_w[l], "q_norm_w": q_norm_w[l], "k_norm_w": k_norm_w[l],
        "A_log": A_log[l], "dt_bias": dt_bias[l], "dn_norm_w": dn_norm_w[l],
        "w_out": w_out[l].astype(BF16), "norm2_w": norm2_w[l],
        "router_w": router_w[l], "router_b": router_b[l],
        "w1": w1[l].astype(BF16), "b1": b1[l], "w2": w2[l].astype(BF16), "b2": b2[l],
    }
    return (_layer(x_prompt, p), _layer(x_sample, p))
```

```python
import functools
import math

import numpy as np
import jax
import jax.numpy as jnp
from jax import lax
from jax.experimental import pallas as pl
from jax.experimental.pallas import tpu as pltpu

F32 = jnp.float32
BF16 = jnp.bfloat16

D_MODEL = 1024
ATTN_W = 512
HEAD_DIM = 64
N_Q_HEADS = 8
N_KV_HEADS = 2
KV_W = 128
DN_W = 512
DN_HEAD_DIM = 128
N_DN_HEADS = 4
CONV_W = 5
CHUNK = 64
GRID_W = 64
ROPE_THETA = 10000.0
N_EXPERTS = 32
TOP_K = 4
D_FF = 1024
SWIGLU_LIMIT = 7.0
SWIGLU_ALPHA = 1.702
NORM_EPS = 1e-6

LANES = 128
PROJ_PAD = 2944
VMEM_LIMIT = 56 * 1024 * 1024

_C_Q = 0
_C_KV = ATTN_W
_C_DN = _C_KV + 2 * KV_W
_C_Z = _C_DN + 3 * DN_W
_C_BA = _C_Z + DN_W


def _cparams(sem):
    return pltpu.CompilerParams(dimension_semantics=sem, vmem_limit_bytes=VMEM_LIMIT)


def _rope_tables(T):
    t = np.arange(T)
    row = (t // GRID_W).astype(np.float64)
    col = (t % GRID_W).astype(np.float64)
    half = HEAD_DIM // 2
    freqs = ROPE_THETA ** (-np.arange(0, half, 2, dtype=np.float64) / half)
    lane = np.arange(LANES)
    d = lane % HEAD_DIM
    use_col = (d // half) == 1
    f = d % (half // 2)
    first = (d % half) < (half // 2)
    pos = np.where(use_col[None, :], col[:, None], row[:, None])
    ang = pos * freqs[f][None, :]
    cos = np.cos(ang)
    sin = np.sin(ang)
    sin_a = np.where(first[None, :], -sin, 0.0)
    sin_b = np.where(first[None, :], 0.0, sin)
    return (jnp.asarray(cos, F32), jnp.asarray(sin_a, F32), jnp.asarray(sin_b, F32))


def _head_mean_matrix():
    lane = np.arange(LANES)
    m = (lane[:, None] // HEAD_DIM == lane[None, :] // HEAD_DIM).astype(np.float32) / HEAD_DIM
    return jnp.asarray(m, BF16)


def _split_bf16(x):
    hi = x.astype(BF16)
    lo = (x - hi.astype(F32)).astype(BF16)
    return hi, lo


def _inproj_kernel(x_ref, n1w_ref, w_ref, qnw_ref, knw_ref, cos_ref, sa_ref, sb_ref, hm_ref,
                   q_ref, k_ref, v_ref, dn_ref, z_ref, ba_ref):
    x = x_ref[...]
    ms = jnp.mean(x * x, axis=-1, keepdims=True)
    hn = (x * lax.rsqrt(ms + NORM_EPS) * n1w_ref[...]).astype(BF16)

    def proj(a, b):
        return jnp.dot(hn, w_ref[:, a:b], preferred_element_type=F32)

    hm = hm_ref[...]
    cos = cos_ref[...]
    sa = sa_ref[...]
    sb = sb_ref[...]
    lane = lax.broadcasted_iota(jnp.int32, (1, LANES), 1)
    lo_half = lane < HEAD_DIM

    def norm_rope(xs, w):
        hi, lo = _split_bf16(xs * xs)
        msq = (jnp.dot(hi, hm, preferred_element_type=F32)
               + jnp.dot(lo, hm, preferred_element_type=F32))
        xn = xs * lax.rsqrt(msq + NORM_EPS) * w
        return (xn * cos + pltpu.roll(xn, LANES - 16, 1) * sa + pltpu.roll(xn, 16, 1) * sb)

    q = proj(_C_Q, _C_Q + ATTN_W)
    scale = HEAD_DIM ** -0.5 * math.log2(math.e)
    for j in range(ATTN_W // LANES):
        qr = norm_rope(q[:, j * LANES:(j + 1) * LANES], qnw_ref[...] * scale)
        qs = pltpu.roll(qr, HEAD_DIM, 1)
        g = (2 * j) // (N_Q_HEADS // N_KV_HEADS)
        keep = lo_half if g == 0 else jnp.logical_not(lo_half)
        h0 = qr if g == 0 else qs
        h1 = qs if g == 0 else qr
        q_ref[:, (2 * j) * LANES:(2 * j + 1) * LANES] = jnp.where(keep, h0, 0.0).astype(BF16)
        q_ref[:, (2 * j + 1) * LANES:(2 * j + 2) * LANES] = jnp.where(keep, h1, 0.0).astype(BF16)

    kv = proj(_C_KV, _C_KV + 2 * KV_W)
    k_ref[...] = norm_rope(kv[:, :KV_W], knw_ref[...]).astype(BF16)
    v_ref[...] = kv[:, KV_W:].astype(BF16)
    dn_ref[...] = proj(_C_DN, _C_DN + 3 * DN_W).astype(BF16)
    z_ref[...] = proj(_C_Z, _C_Z + DN_W).astype(BF16)
    ba_ref[...] = proj(_C_BA, _C_BA + LANES)[:, :4 * N_DN_HEADS]


def _inproj(x2d, T, norm1_w, w_in_b, q_norm_w, k_norm_w, tm=512):
    n = x2d.shape[0]
    cos, sa, sb = _rope_tables(T)
    tm = min(tm, T)
    tpb = T // tm
    tab_spec = pl.BlockSpec((tm, LANES), lambda i: (i % tpb, 0))
    const = lambda shape: pl.BlockSpec(shape, lambda i: (0,) * len(shape))
    row = lambda w: pl.BlockSpec((tm, w), lambda i: (i, 0))
    qnw = jnp.tile(q_norm_w, LANES // HEAD_DIM)[None, :]
    knw = jnp.tile(k_norm_w, LANES // HEAD_DIM)[None, :]
    return pl.pallas_call(
        _inproj_kernel,
        grid=(n // tm,),
        in_specs=[row(D_MODEL), const((1, D_MODEL)), const((D_MODEL, PROJ_PAD)),
                  const((1, LANES)), const((1, LANES)), tab_spec, tab_spec, tab_spec,
                  const((LANES, LANES))],
        out_specs=[row(N_Q_HEADS * LANES), row(KV_W), row(KV_W), row(3 * DN_W), row(DN_W),
                   row(4 * N_DN_HEADS)],
        out_shape=[jax.ShapeDtypeStruct((n, N_Q_HEADS * LANES), BF16),
                   jax.ShapeDtypeStruct((n, KV_W), BF16),
                   jax.ShapeDtypeStruct((n, KV_W), BF16),
                   jax.ShapeDtypeStruct((n, 3 * DN_W), BF16),
                   jax.ShapeDtypeStruct((n, DN_W), BF16),
                   jax.ShapeDtypeStruct((n, 4 * N_DN_HEADS), F32)],
        compiler_params=_cparams(("parallel",)),
        name="inproj",
    )(x2d, norm1_w[None, :], w_in_b, qnw, knw, cos, sa, sb, _head_mean_matrix())


_ATTN_PARTS = 4


def _attn_kernel(q_ref, k_ref, v_ref, o_ref, m_sc, acc_sc, *, tq, tk, n_kc):
    hp = N_Q_HEADS // _ATTN_PARTS
    rows = hp * tq
    qs = [jnp.concatenate([q_ref[0, :, h * LANES:(h + 1) * LANES] for h in range(u * hp, (u + 1) * hp)],
                          axis=0) for u in range(_ATTN_PARTS)]
    m_sc[...] = jnp.full(m_sc.shape, -jnp.inf, F32)
    acc_sc[...] = jnp.zeros(acc_sc.shape, F32)
    ones = jnp.ones((tk, LANES), BF16)

    def scores(start, u):
        kc = k_ref[0, pl.ds(start, tk), :]
        return lax.dot_general(qs[u], kc, (((1,), (1,)), ((), ())), preferred_element_type=F32)

    def softmax_pv(s, start, u):
        r = slice(u * rows, (u + 1) * rows)
        m_prev = m_sc[r, :]
        m_next = jnp.maximum(m_prev, jnp.max(s, axis=1, keepdims=True))
        alpha = jnp.exp2(m_prev - m_next)
        p = jnp.concatenate(
            [jnp.exp2(s[:, j * LANES:(j + 1) * LANES] - m_next) for j in range(tk // LANES)], axis=1)
        v_aug = jnp.concatenate([v_ref[0, pl.ds(start, tk), :], ones], axis=1)
        pv = jnp.dot(p.astype(BF16), v_aug, preferred_element_type=F32)
        acc_sc[r, :] = acc_sc[r, :] * jnp.concatenate([alpha, alpha], axis=1) + pv
        m_sc[r, :] = m_next

    unroll = 4 if n_kc % 4 == 0 else (2 if n_kc % 2 == 0 else 1)

    def body(c, carry):
        units = [(pl.multiple_of((c * unroll + cc) * tk, tk), u)
                 for cc in range(unroll) for u in range(_ATTN_PARTS)]
        s = {i: scores(*units[i]) for i in range(min(2, len(units)))}
        for i, (start, u) in enumerate(units):
            softmax_pv(s.pop(i), start, u)
            if i + 2 < len(units):
                s[i + 2] = scores(*units[i + 2])
        return carry

    lax.fori_loop(0, n_kc // unroll, body, 0)

    lane = lax.broadcasted_iota(jnp.int32, (1, LANES), 1)
    lo_half = lane < HEAD_DIM
    outs = []
    for h in range(N_Q_HEADS):
        a = acc_sc[h * tq:(h + 1) * tq, :]
        outs.append(a[:, :LANES] / a[:, LANES:])
    for j in range(N_Q_HEADS // 2):
        g = (2 * j) // (N_Q_HEADS // N_KV_HEADS)
        a, b = outs[2 * j], outs[2 * j + 1]
        if g == 0:
            b = pltpu.roll(b, HEAD_DIM, 1)
        else:
            a = pltpu.roll(a, HEAD_DIM, 1)
        o_ref[0, :, j * LANES:(j + 1) * LANES] = jnp.where(lo_half, a, b).astype(BF16)


def _attention(q, k, v, tq=128, tk=512):
    B, T, _ = q.shape
    tk = min(tk, T)
    kern = functools.partial(_attn_kernel, tq=tq, tk=tk, n_kc=T // tk)
    return pl.pallas_call(
        kern,
        grid=(B, T // tq),
        in_specs=[pl.BlockSpec((1, tq, N_Q_HEADS * LANES), lambda b, i: (b, i, 0)),
                  pl.BlockSpec((1, T, KV_W), lambda b, i: (b, 0, 0)),
                  pl.BlockSpec((1, T, KV_W), lambda b, i: (b, 0, 0))],
        out_specs=pl.BlockSpec((1, tq, ATTN_W), lambda b, i: (b, i, 0)),
        out_shape=jax.ShapeDtypeStruct((B, T, ATTN_W), BF16),
        scratch_shapes=[pltpu.VMEM((N_Q_HEADS * tq, LANES), F32),
                        pltpu.VMEM((N_Q_HEADS * tq, 2 * LANES), F32)],
        compiler_params=_cparams(("parallel", "parallel")),
        name="attention",
    )(q, k, v)


_GROUP = 8


def _dn_masks():
    r = lax.broadcasted_iota(jnp.int32, (CHUNK, CHUNK), 0)
    c = lax.broadcasted_iota(jnp.int32, (CHUNK, CHUNK), 1)
    return r, c


def _dn_kernel(xq_ref, xk_ref, xv_ref, z_ref, ba_ref, cw_ref, ad_ref, nw_ref, o_ref,
               xpad, qn_sc, kn_sc, vn_sc, u_sc, wq_sc, w2_sc, b_sc, qk_sc, gt_sc, s_sc, of_sc, ob_sc,
               *, T):
    n_chunks = T // CHUNK
    n_groups = n_chunks // _GROUP
    gsz = _GROUP * CHUNK
    h = pl.program_id(1)

    def conv_pass(x_ref, w_idx, dst, mode):
        xpad[0:8, :] = jnp.zeros((8, LANES), F32)
        xpad[T + 8:T + 16, :] = jnp.zeros((8, LANES), F32)
        blk = 256

        def cp(i, c):
            s = pl.multiple_of(i * blk, blk)
            xpad[pl.ds(s + 8, blk), :] = x_ref[0, pl.ds(s, blk), :].astype(F32)
            return c
        lax.fori_loop(0, T // blk, cp, 0)
        w = cw_ref[w_idx]

        def cv(i, c):
            s = pl.multiple_of(i * blk, blk)
            acc = None
            for j in range(CONV_W):
                term = xpad[pl.ds(s + 8 + j - CONV_W // 2, blk), :] * w[j:j + 1, :]
                acc = term if acc is None else acc + term
            y = acc / (1.0 + jnp.exp(-acc))
            if mode == "q":
                y = y * lax.rsqrt(jnp.sum(y * y, axis=-1, keepdims=True) + NORM_EPS) * (DN_HEAD_DIM ** -0.5)
            elif mode == "k":
                y = y * lax.rsqrt(jnp.sum(y * y, axis=-1, keepdims=True) + NORM_EPS)
            dst[pl.ds(s, blk), :] = y
            return c
        lax.fori_loop(0, T // blk, cv, 0)

    conv_pass(xq_ref, 0, qn_sc, "q")
    conv_pass(xk_ref, 1, kn_sc, "k")
    conv_pass(xv_ref, 2, vn_sc, "v")

    r, c = _dn_masks()
    cum_ops = [jnp.concatenate([(c <= r).astype(BF16), (c > r).astype(BF16)], axis=0),
               jnp.concatenate([(c >= r).astype(BF16), (c < r).astype(BF16)], axis=0)]
    r2 = lax.broadcasted_iota(jnp.int32, (CHUNK, LANES), 0)
    c2 = lax.broadcasted_iota(jnp.int32, (CHUNK, LANES), 1)
    lo = c2 < CHUNK
    cc = jnp.where(lo, c2, c2 - CHUNK)
    gs_zero = [jnp.logical_and(lo, r2 <= cc), jnp.logical_and(jnp.logical_not(lo), r2 >= cc)]
    strict_p = jnp.where(lo, r2 - cc, cc - r2) > 0
    incl_p = jnp.where(lo, r2 - cc, cc - r2) >= 0
    eye_p = (cc == r2).astype(F32)
    zeros_c = jnp.zeros((CHUNK, LANES), F32)
    nt = (((1,), (1,)), ((), ()))
    tn = (((0,), (0,)), ((), ()))
    top, bot = slice(0, CHUNK), slice(CHUNK, 2 * CHUNK)
    dirs = range(2)

    def block_diag(p):
        return jnp.concatenate([jnp.where(lo, p, 0.0), jnp.where(lo, 0.0, p)], axis=0).astype(BF16)

    def gates(tok0, d):
        ba = ba_ref[0, 0, pl.ds(tok0, CHUNK), :]
        beta = 1.0 / (1.0 + jnp.exp(-ba[:, d:d + 1]))
        a_in = ba[:, 2 + d:3 + d] + ad_ref[0, 2 + d:3 + d, :][:, 0:1]
        sp = jnp.maximum(a_in, 0.0) + jnp.log(1.0 + jnp.exp(-jnp.abs(a_in)))
        return beta, -jnp.exp(ad_ref[0, d:d + 1, :][:, 0:1]) * sp

    def group_toks(gi):
        hint = (lambda x, m: x) if isinstance(gi, int) else pl.multiple_of
        base_f = hint(gi * gsz, gsz)
        base_b = hint((n_groups - 1 - gi) * gsz, gsz)
        return [(hint(base_f + j * CHUNK, CHUNK), hint(base_b + (_GROUP - 1 - j) * CHUNK, CHUNK))
                for j in range(_GROUP)]

    def prep_group(toks, buf):
        n = len(toks)
        q = [[qn_sc[pl.ds(t[d], CHUNK), :] for d in dirs] for t in toks]
        k = [[kn_sc[pl.ds(t[d], CHUNK), :] for d in dirs] for t in toks]
        v = [[vn_sc[pl.ds(t[d], CHUNK), :] for d in dirs] for t in toks]
        bg = [[gates(t[d], d) for d in dirs] for t in toks]
        cs = []
        for j in range(n):
            row = []
            for d in dirs:
                rhs = jnp.where(gs_zero[d], 0.0, jnp.broadcast_to(bg[j][d][1], (CHUNK, LANES)))
                m = jnp.dot(cum_ops[d], jnp.concatenate(_split_bf16(rhs), axis=1),
                            preferred_element_type=F32)
                row.append(m[:, :LANES] + m[:, LANES:])
            cs.append(row)
        yield
        own = [CHUNK, 0]
        gc = [[cs[j][d][top, own[d]:own[d] + 1] for d in dirs] for j in range(n)]
        grem = [[cs[j][d][bot, own[d]:own[d] + 1] for d in dirs] for j in range(n)]
        decay = [jnp.exp(jnp.where(lo, cs[j][0][top], cs[j][1][top])) for j in range(n)]
        egc = [[jnp.exp(gc[j][d]) for d in dirs] for j in range(n)]
        kb = [[k[j][d] * bg[j][d][0] for d in dirs] for j in range(n)]
        kq = []
        for j in range(n):
            kcat = jnp.concatenate([k[j][0], k[j][1]], axis=0).astype(BF16)
            kq.append([lax.dot_general(jnp.concatenate([kb[j][d], q[j][d]], axis=0).astype(BF16), kcat,
                                       nt, preferred_element_type=F32) for d in dirs])
        yield
        a = [jnp.where(strict_p, jnp.where(lo, kq[j][0][top], kq[j][1][top]) * decay[j], 0.0)
             for j in range(n)]
        for j in range(n):
            qk = jnp.where(incl_p, jnp.where(lo, kq[j][0][bot], kq[j][1][bot]) * decay[j], 0.0)
            qk_sc[buf, pl.ds(j * CHUNK, CHUNK), :] = qk.astype(BF16)
        x = [eye_p - a[j] for j in range(n)]
        p = a
        pbd = [block_diag(p[j]) for j in range(n)]
        for _ in range(5):
            p = [jnp.dot(p[j].astype(BF16), pbd[j], preferred_element_type=F32) for j in range(n)]
            yield
            pbd = [block_diag(p[j]) for j in range(n)]
            x = [x[j] + jnp.dot(x[j].astype(BF16), pbd[j], preferred_element_type=F32) for j in range(n)]
            yield
        uw = []
        for j in range(n):
            r_f = jnp.concatenate([v[j][0] * bg[j][0][0], kb[j][0] * egc[j][0], zeros_c, zeros_c], axis=1)
            r_b = jnp.concatenate([zeros_c, zeros_c, v[j][1] * bg[j][1][0], kb[j][1] * egc[j][1]], axis=1)
            uw.append(jnp.dot(x[j].astype(BF16), jnp.concatenate([r_f, r_b], axis=0).astype(BF16),
                              preferred_element_type=F32))
        yield
        for j in range(n):
            for d in dirs:
                u = uw[j][:, 2 * d * LANES:(2 * d + 1) * LANES]
                w = uw[j][:, (2 * d + 1) * LANES:(2 * d + 2) * LANES]
                kd = (k[j][d] * jnp.exp(grem[j][d])).astype(BF16)
                w2b = lax.dot_general(kd, jnp.concatenate([w, u], axis=1).astype(BF16), tn,
                                      preferred_element_type=F32)
                blk = pl.ds(j * 2 * CHUNK, 2 * CHUNK)
                bd = buf * 2 + d
                u_sc[bd, pl.ds(j * CHUNK, CHUNK), :] = u
                wq_sc[bd, blk, :] = jnp.concatenate([w, q[j][d] * egc[j][d]], axis=0).astype(BF16)
                w2_sc[bd, blk, :] = w2b[:, :LANES].astype(BF16)
                b_sc[bd, blk, :] = w2b[:, LANES:]
                tot = gc[j][d][CHUNK - 1:CHUNK, :] if d == 0 else gc[j][d][0:1, :]
                gt_sc[bd, pl.ds(j * 8, 8), :] = jnp.broadcast_to(jnp.exp(tot), (8, LANES))
        yield

    def recur_group(buf, toks):
        s = [s_sc[0], s_sc[1]]
        for j in range(_GROUP):
            blk = pl.ds(j * 2 * CHUNK, 2 * CHUNK)
            ws, sw = [], []
            for d in dirs:
                sb = s[d].astype(BF16)
                ws.append(jnp.dot(w2_sc[buf * 2 + d, blk, :], sb, preferred_element_type=F32))
                sw.append(jnp.dot(wq_sc[buf * 2 + d, blk, :], sb, preferred_element_type=F32))
                yield
            s = [s[d] * gt_sc[buf * 2 + d, pl.ds(j * 8, 1), :] + b_sc[buf * 2 + d, blk, :] - ws[d] for d in dirs]
            vn = [u_sc[buf * 2 + d, pl.ds(j * CHUNK, CHUNK), :] - sw[d][top] for d in dirs]
            vbd = jnp.concatenate([jnp.concatenate([vn[0], zeros_c], axis=1),
                                   jnp.concatenate([zeros_c, vn[1]], axis=1)], axis=0).astype(BF16)
            oo = jnp.dot(qk_sc[buf, pl.ds(j * CHUNK, CHUNK), :], vbd, preferred_element_type=F32)
            of_sc[pl.ds(toks[j][0], CHUNK), :] = sw[0][bot] + oo[:, :LANES]
            ob_sc[pl.ds(toks[j][1], CHUNK), :] = sw[1][bot] + oo[:, LANES:]
            yield
        s_sc[0] = s[0]
        s_sc[1] = s[1]

    n_stage = 14
    n_piece = 3 * _GROUP

    for _ in prep_group(group_toks(0), 0):
        pass
    s_sc[...] = jnp.zeros(s_sc.shape, F32)

    def group_body(gi, carry):
        buf = gi % 2
        nxt = prep_group(group_toks(gi + 1), 1 - buf)
        cur = recur_group(buf, group_toks(gi))
        done = 0
        for st in range(n_stage):
            next(nxt)
            want = -(-n_piece * (st + 1) // n_stage)
            for _ in range(want - done):
                next(cur)
            done = want
        for _ in cur:
            pass
        return carry

    lax.fori_loop(0, n_groups - 1, group_body, 0)
    for _ in recur_group((n_groups - 1) % 2, group_toks(n_groups - 1)):
        pass

    blk = 256

    def fin(i, carry):
        s = pl.multiple_of(i * blk, blk)
        o = of_sc[pl.ds(s, blk), :] + ob_sc[pl.ds(s, blk), :]
        o = o * lax.rsqrt(jnp.mean(o * o, axis=-1, keepdims=True) + NORM_EPS) * nw_ref[...]
        z = z_ref[0, pl.ds(s, blk), :].astype(F32)
        o_ref[0, pl.ds(s, blk), :] = (o * (z / (1.0 + jnp.exp(-z)))).astype(BF16)
        return carry
    lax.fori_loop(0, T // blk, fin, 0)


def _deltanet(dn, z, ba, conv_w, A_log, dt_bias, dn_norm_w):
    B, T, _ = dn.shape
    H = N_DN_HEADS
    ba_t = ba.reshape(B, T, 2, 2, H).transpose(0, 4, 1, 2, 3).reshape(B, H, T, 4)
    cw = jnp.pad(conv_w, ((0, 8 - CONV_W), (0, 0))).reshape(8, 3, H, DN_HEAD_DIM).transpose(2, 1, 0, 3)
    cw = cw.reshape(H * 3, 8, DN_HEAD_DIM)
    ad = jnp.concatenate([A_log, dt_bias], axis=0).T
    ad = jnp.broadcast_to(ad[:, :, None], (H, 4, LANES))
    ad = jnp.pad(ad, ((0, 0), (0, 4), (0, 0)))
    gsz = _GROUP * CHUNK
    kern = functools.partial(_dn_kernel, T=T)
    col = lambda off: pl.BlockSpec((1, T, DN_HEAD_DIM), lambda b, h, off=off: (b, 0, off + h))
    return pl.pallas_call(
        kern,
        grid=(B, H),
        in_specs=[col(0), col(H), col(2 * H),
                  pl.BlockSpec((1, T, DN_HEAD_DIM), lambda b, h: (b, 0, h)),
                  pl.BlockSpec((1, 1, T, 4), lambda b, h: (b, h, 0, 0)),
                  pl.BlockSpec((3, 8, DN_HEAD_DIM), lambda b, h: (h, 0, 0)),
                  pl.BlockSpec((1, 8, LANES), lambda b, h: (h, 0, 0)),
                  pl.BlockSpec((1, DN_HEAD_DIM), lambda b, h: (0, 0))],
        out_specs=pl.BlockSpec((1, T, DN_HEAD_DIM), lambda b, h: (b, 0, h)),
        out_shape=jax.ShapeDtypeStruct((B, T, DN_W), BF16),
        scratch_shapes=[pltpu.VMEM((T + 16, LANES), F32),
                        pltpu.VMEM((T, LANES), F32),
                        pltpu.VMEM((T, LANES), F32),
                        pltpu.VMEM((T, LANES), F32),
                        pltpu.VMEM((4, gsz, LANES), F32),
                        pltpu.VMEM((4, 2 * gsz, LANES), BF16),
                        pltpu.VMEM((4, 2 * gsz, LANES), BF16),
                        pltpu.VMEM((4, 2 * gsz, LANES), F32),
                        pltpu.VMEM((2, gsz, LANES), BF16),
                        pltpu.VMEM((4, _GROUP * 8, LANES), F32),
                        pltpu.VMEM((2, DN_HEAD_DIM, DN_HEAD_DIM), F32),
                        pltpu.VMEM((T, LANES), F32),
                        pltpu.VMEM((T, LANES), F32)],
        compiler_params=_cparams(("parallel", "parallel")),
        name="deltanet",
    )(dn, dn, dn, z, ba_t, cw, ad, dn_norm_w[None, :])


def _outproj_kernel(x_ref, at_ref, dn_ref, wo_ref, n2w_ref, rw_ref, rb_ref,
                    h_ref, h2_ref, idx_ref, gate_ref):
    mix = (jnp.dot(at_ref[...], wo_ref[0:ATTN_W, :], preferred_element_type=F32)
           + jnp.dot(dn_ref[...], wo_ref[ATTN_W:, :], preferred_element_type=F32))
    hres = x_ref[...] + mix
    h_ref[...] = hres
    h2 = hres * lax.rsqrt(jnp.mean(hres * hres, axis=-1, keepdims=True) + NORM_EPS) * n2w_ref[...]
    h2_ref[...] = h2
    h2_hi, h2_lo = _split_bf16(h2)
    hh = jnp.dot(h2_hi, rw_ref[...], preferred_element_type=F32)
    logits = (hh[:, :LANES] + hh[:, LANES:] + jnp.dot(h2_lo, rw_ref[:, :LANES], preferred_element_type=F32)
              + rb_ref[...])
    lane = lax.broadcasted_iota(jnp.int32, logits.shape, 1).astype(F32)
    vals = []
    idxs = []
    l = logits
    for _ in range(TOP_K):
        m = jnp.max(l, axis=-1, keepdims=True)
        i = jnp.min(jnp.where(l == m, lane, float(LANES)), axis=-1, keepdims=True)
        vals.append(m)
        idxs.append(i)
        l = jnp.where(lane == i, -jnp.inf, l)
    es = [jnp.exp(v - vals[0]) for v in vals]
    den = es[0] + es[1] + es[2] + es[3]
    idx_ref[...] = jnp.concatenate(idxs, axis=1).astype(jnp.int32)
    gate_ref[...] = jnp.concatenate([e / den for e in es], axis=1)


def _outproj_router(x2d, attn2d, dn2d, w_out_b, norm2_w, router_w, router_b, tm=512):
    n = x2d.shape[0]
    rw = jnp.concatenate(_split_bf16(jnp.pad(router_w, ((0, 0), (0, LANES - N_EXPERTS)))), axis=1)
    rb = jnp.pad(router_b, (0, LANES - N_EXPERTS), constant_values=-jnp.inf)[None, :]
    const = lambda shape: pl.BlockSpec(shape, lambda i: (0,) * len(shape))
    row = lambda w: pl.BlockSpec((tm, w), lambda i: (i, 0))
    return pl.pallas_call(
        _outproj_kernel,
        grid=(n // tm,),
        in_specs=[row(D_MODEL), row(ATTN_W), row(DN_W), const((D_MODEL, D_MODEL)),
                  const((1, D_MODEL)), const((D_MODEL, 2 * LANES)), const((1, LANES))],
        out_specs=[row(D_MODEL), row(D_MODEL), row(TOP_K), row(TOP_K)],
        out_shape=[jax.ShapeDtypeStruct((n, D_MODEL), F32),
                   jax.ShapeDtypeStruct((n, D_MODEL), F32),
                   jax.ShapeDtypeStruct((n, TOP_K), jnp.int32),
                   jax.ShapeDtypeStruct((n, TOP_K), F32)],
        compiler_params=_cparams(("parallel",)),
        name="outproj_router",
    )(x2d, attn2d, dn2d, w_out_b, norm2_w[None, :], rw, rb)


_BM = 256


def _moe_kernel(be_ref, nu_ref, tok_ref, tokn_ref, dstp_ref, dstc_ref, h2_hbm, w1_ref, b1_ref, w2_ref,
                b2_ref, yk_hbm, xbuf, ybuf, act_sc, gsem, ssem, *, n_assign):
    i = pl.program_id(0)
    n_used = nu_ref[0]
    nc = 256

    def gather_row(idx_ref, sl, rr):
        return pltpu.make_async_copy(h2_hbm.at[pl.ds(idx_ref[0, 0, rr], 1)], xbuf.at[sl, pl.ds(rr, 1)],
                                     gsem.at[sl])

    def scatter_row(dst_ref, sl, rr):
        return pltpu.make_async_copy(ybuf.at[sl, pl.ds(rr, 1)], yk_hbm.at[pl.ds(dst_ref[0, 0, rr], 1)],
                                     ssem.at[sl])

    def wait_gather(sl):
        pltpu.make_async_copy(h2_hbm.at[pl.ds(0, _BM)], xbuf.at[sl], gsem.at[sl]).wait()

    def wait_scatter(sl):
        pltpu.make_async_copy(ybuf.at[sl], yk_hbm.at[pl.ds(0, _BM)], ssem.at[sl]).wait()

    @pl.when(i == 0)
    def _():
        ybuf[...] = jnp.zeros(ybuf.shape, F32)
        pltpu.make_async_copy(ybuf.at[0], yk_hbm.at[pl.ds(n_assign, _BM)], ssem.at[0]).start()

        def body(rr, c):
            gather_row(tok_ref, 0, rr).start()
            return c
        lax.fori_loop(0, _BM, body, 0, unroll=8)

    def step(slot, other):
        wait_gather(slot)
        jobs = ([functools.partial(gather_row, tokn_ref, other, rr) for rr in range(_BM)]
                + [functools.partial(scatter_row, dstp_ref, other, rr) for rr in range(_BM)])
        n_gaps = D_FF // nc + D_MODEL // nc
        per_gap = -(-len(jobs) // n_gaps)

        def issue(gap):
            for n, job in enumerate(jobs[gap * per_gap:(gap + 1) * per_gap]):
                job().start(priority=n % 2)

        x = xbuf[slot].astype(BF16)
        for j in range(D_FF // nc):
            cg = slice(j * nc, (j + 1) * nc)
            cu = slice(D_FF + j * nc, D_FF + (j + 1) * nc)
            gate = jnp.dot(x, w1_ref[0, :, cg].astype(BF16), preferred_element_type=F32) + b1_ref[0, :, cg]
            up = jnp.dot(x, w1_ref[0, :, cu].astype(BF16), preferred_element_type=F32) + b1_ref[0, :, cu]
            gate = jnp.minimum(gate, SWIGLU_LIMIT)
            up = jnp.clip(up, -SWIGLU_LIMIT, SWIGLU_LIMIT)
            act = gate * (1.0 / (1.0 + jnp.exp(-SWIGLU_ALPHA * gate))) * (up + 1.0)
            act_sc[:, cg] = act.astype(BF16)
            issue(j)
        wait_scatter(slot)
        a = act_sc[...]
        for j in range(D_MODEL // nc):
            cn = slice(j * nc, (j + 1) * nc)
            ybuf[slot, :, cn] = (jnp.dot(a, w2_ref[0, :, cn].astype(BF16), preferred_element_type=F32)
                                 + b2_ref[0, :, cn])
            issue(D_FF // nc + j)

        @pl.when(i == n_used - 1)
        def _():
            def body(rr, c):
                scatter_row(dstc_ref, slot, rr).start()
                return c
            lax.fori_loop(0, _BM, body, 0, unroll=8)
            wait_scatter(other)
            wait_scatter(slot)
            wait_gather(other)

    for parity in range(2):
        pl.when(jnp.logical_and(i < n_used, i % 2 == parity))(functools.partial(step, parity, 1 - parity))


def _moe_blocks(n_tok):
    n_assign = n_tok * TOP_K
    return n_assign // _BM + N_EXPERTS


def _routing_plan(idx):
    n_tok = idx.shape[0]
    n_assign = n_tok * TOP_K
    nb = _moe_blocks(n_tok)
    flat_e = idx.reshape(-1)
    order = jnp.argsort(flat_e, stable=True).astype(jnp.int32)
    counts = jnp.sum(flat_e[:, None] == jnp.arange(N_EXPERTS, dtype=jnp.int32)[None, :], axis=0,
                     dtype=jnp.int32)
    padded = (counts + _BM - 1) // _BM * _BM
    start = jnp.cumsum(counts) - counts
    pad_end = jnp.cumsum(padded)
    pad_start = pad_end - padded
    n_used = (pad_end[-1] // _BM).astype(jnp.int32)
    bidx = jnp.arange(nb, dtype=jnp.int32)
    blk = jnp.minimum(bidx, n_used - 1) * _BM
    block_e = jnp.minimum(jnp.sum(pad_end[None, :] <= blk[:, None], axis=1), N_EXPERTS - 1).astype(jnp.int32)
    lane = jnp.arange(_BM, dtype=jnp.int32)[None, :]
    rank = blk[:, None] - pad_start[block_e][:, None] + lane
    valid = rank < counts[block_e][:, None]
    src = jnp.clip(start[block_e][:, None] + rank, 0, n_assign - 1)
    assign = jnp.where(valid, order[src], 0)
    row_tok = assign // TOP_K
    dump = n_assign + (bidx[:, None] % 2) * _BM + lane
    row_dst = jnp.where(valid, (assign % TOP_K) * n_tok + row_tok, dump)
    row_dst = jnp.concatenate([n_assign + _BM + lane, row_dst], axis=0)
    return (block_e, n_used.reshape(1), row_tok.reshape(nb, 1, _BM), row_dst.reshape(nb + 1, 1, _BM))


def _moe(h2, idx, w1_b, b1, w2_b, b2):
    n_tok = h2.shape[0]
    n_assign = n_tok * TOP_K
    nb = _moe_blocks(n_tok)
    block_e, n_used, row_tok, row_dst = _routing_plan(idx)
    smem_blk = lambda f: pl.BlockSpec((1, 1, _BM), f, memory_space=pltpu.SMEM)
    grid_spec = pltpu.PrefetchScalarGridSpec(
        num_scalar_prefetch=2,
        grid=(nb,),
        in_specs=[smem_blk(lambda i, be, nu: (i, 0, 0)),
                  smem_blk(lambda i, be, nu: (jnp.minimum(i + 1, nb - 1), 0, 0)),
                  smem_blk(lambda i, be, nu: (i, 0, 0)),
                  smem_blk(lambda i, be, nu: (i + 1, 0, 0)),
                  pl.BlockSpec(memory_space=pl.ANY),
                  pl.BlockSpec((1, D_MODEL, 2 * D_FF), lambda i, be, nu: (be[i], 0, 0)),
                  pl.BlockSpec((1, 1, 2 * D_FF), lambda i, be, nu: (be[i], 0, 0)),
                  pl.BlockSpec((1, D_FF, D_MODEL), lambda i, be, nu: (be[i], 0, 0)),
                  pl.BlockSpec((1, 1, D_MODEL), lambda i, be, nu: (be[i], 0, 0))],
        out_specs=pl.BlockSpec(memory_space=pl.ANY),
        scratch_shapes=[pltpu.VMEM((2, _BM, D_MODEL), F32),
                        pltpu.VMEM((2, _BM, D_MODEL), F32),
                        pltpu.VMEM((_BM, D_FF), BF16),
                        pltpu.SemaphoreType.DMA((2,)),
                        pltpu.SemaphoreType.DMA((2,))],
    )
    return pl.pallas_call(
        functools.partial(_moe_kernel, n_assign=n_assign),
        grid_spec=grid_spec,
        out_shape=jax.ShapeDtypeStruct((n_assign + 2 * _BM, D_MODEL), F32),
        compiler_params=_cparams(("arbitrary",)),
        name="moe_experts",
    )(block_e, n_used, row_tok, row_tok, row_dst, row_dst, h2, w1_b, b1[:, None, :], w2_b, b2[:, None, :])


def _combine_kernel(h_ref, y0_ref, y1_ref, y2_ref, y3_ref, g_ref, o_ref):
    g = g_ref[...]
    acc = h_ref[...]
    for k, y_ref in enumerate((y0_ref, y1_ref, y2_ref, y3_ref)):
        acc = acc + g[:, k:k + 1] * y_ref[...]
    o_ref[...] = acc


def _combine(h, yk, gates, tm=256):
    n = h.shape[0]
    nt = n // tm
    row = lambda w: pl.BlockSpec((tm, w), lambda i: (i, 0))
    ysp = lambda k: pl.BlockSpec((tm, D_MODEL), lambda i, k=k: (k * nt + i, 0))
    return pl.pallas_call(
        _combine_kernel,
        grid=(nt,),
        in_specs=[row(D_MODEL), ysp(0), ysp(1), ysp(2), ysp(3), row(TOP_K)],
        out_specs=row(D_MODEL),
        out_shape=jax.ShapeDtypeStruct((n, D_MODEL), F32),
        compiler_params=_cparams(("parallel",)),
        name="moe_combine",
    )(h, yk, yk, yk, yk, gates)


def _layer(x, p):
    B, T, D = x.shape
    x2d = x.reshape(B * T, D)
    q, k, v, dn, z, ba = _inproj(x2d, T, p["norm1_w"], p["w_in"], p["q_norm_w"], p["k_norm_w"])
    attn = _attention(q.reshape(B, T, -1), k.reshape(B, T, -1), v.reshape(B, T, -1))
    dno = _deltanet(dn.reshape(B, T, -1), z.reshape(B, T, -1), ba.reshape(B, T, -1),
                    p["conv_w"], p["A_log"], p["dt_bias"], p["dn_norm_w"])
    h, h2, idx, gates = _outproj_router(x2d, attn.reshape(B * T, -1), dno.reshape(B * T, -1),
                                        p["w_out"], p["norm2_w"], p["router_w"], p["router_b"])
    yk = _moe(h2, idx, p["w1"], p["b1"], p["w2"], p["b2"])
    y = _combine(h, yk, gates)
    return y.reshape(B, T, D)


def kernel(x_prompt, x_sample, norm1_w, w_in, conv_w, q_norm_w, k_norm_w, A_log, dt_bias, dn_norm_w,
           w_out, norm2_w, router_w, router_b, w1, b1, w2, b2):
    l = 0
    p = {
        "norm1_w": norm1_w[l],
        "w_in": jnp.pad(w_in[l], ((0, 0), (0, PROJ_PAD - w_in.shape[-1]))).astype(BF16),
        "conv_w": conv_w[l], "q_norm_w": q_norm_w[l], "k_norm_w": k_norm_w[l],
        "A_log": A_log[l], "dt_bias": dt_bias[l], "dn_norm_w": dn_norm_w[l],
        "w_out": w_out[l].astype(BF16), "norm2_w": norm2_w[l],
        "router_w": router_w[l], "router_b": router_b[l],
        "w1": w1[l], "b1": b1[l], "w2": w2[l], "b2": b2[l],
    }
    return (_layer(x_prompt, p), _layer(x_sample, p))
```

```python
import functools
import math

import numpy as np
import jax
import jax.numpy as jnp
from jax import lax
from jax.experimental import pallas as pl
from jax.experimental.pallas import tpu as pltpu

F32 = jnp.float32
BF16 = jnp.bfloat16

D_MODEL = 1024
ATTN_W = 512
HEAD_DIM = 64
N_Q_HEADS = 8
N_KV_HEADS = 2
KV_W = 128
DN_W = 512
DN_HEAD_DIM = 128
N_DN_HEADS = 4
CONV_W = 5
CHUNK = 64
GRID_W = 64
ROPE_THETA = 10000.0
N_EXPERTS = 32
TOP_K = 4
D_FF = 1024
SWIGLU_LIMIT = 7.0
SWIGLU_ALPHA = 1.702
NORM_EPS = 1e-6

LANES = 128
PROJ_PAD = 2944
VMEM_LIMIT = 56 * 1024 * 1024

_C_Q = 0
_C_KV = ATTN_W
_C_DN = _C_KV + 2 * KV_W
_C_Z = _C_DN + 3 * DN_W
_C_BA = _C_Z + DN_W


def _cparams(sem):
    return pltpu.CompilerParams(dimension_semantics=sem, vmem_limit_bytes=VMEM_LIMIT)


def _rope_tables(T):
    t = np.arange(T)
    row = (t // GRID_W).astype(np.float64)
    col = (t % GRID_W).astype(np.float64)
    half = HEAD_DIM // 2
    freqs = ROPE_THETA ** (-np.arange(0, half, 2, dtype=np.float64) / half)
    lane = np.arange(LANES)
    d = lane % HEAD_DIM
    use_col = (d // half) == 1
    f = d % (half // 2)
    first = (d % half) < (half // 2)
    pos = np.where(use_col[None, :], col[:, None], row[:, None])
    ang = pos * freqs[f][None, :]
    cos = np.cos(ang)
    sin = np.sin(ang)
    sin_a = np.where(first[None, :], -sin, 0.0)
    sin_b = np.where(first[None, :], 0.0, sin)
    return (jnp.asarray(cos, F32), jnp.asarray(sin_a, F32), jnp.asarray(sin_b, F32))


def _head_mean_matrix():
    lane = np.arange(LANES)
    m = (lane[:, None] // HEAD_DIM == lane[None, :] // HEAD_DIM).astype(np.float32) / HEAD_DIM
    return jnp.asarray(m, BF16)


def _split_bf16(x):
    hi = x.astype(BF16)
    lo = (x - hi.astype(F32)).astype(BF16)
    return hi, lo


def _inproj_kernel(x_ref, n1w_ref, w_ref, qnw_ref, knw_ref, cos_ref, sa_ref, sb_ref, hm_ref,
                   q_ref, k_ref, v_ref, dn_ref, z_ref, ba_ref):
    x = x_ref[...]
    ms = jnp.mean(x * x, axis=-1, keepdims=True)
    hn = (x * lax.rsqrt(ms + NORM_EPS) * n1w_ref[...]).astype(BF16)

    def proj(a, b):
        return jnp.dot(hn, w_ref[:, a:b], preferred_element_type=F32)

    hm = hm_ref[...]
    cos = cos_ref[...]
    sa = sa_ref[...]
    sb = sb_ref[...]
    lane = lax.broadcasted_iota(jnp.int32, (1, LANES), 1)
    lo_half = lane < HEAD_DIM

    def norm_rope(xs, w):
        hi, lo = _split_bf16(xs * xs)
        msq = (jnp.dot(hi, hm, preferred_element_type=F32)
               + jnp.dot(lo, hm, preferred_element_type=F32))
        xn = xs * lax.rsqrt(msq + NORM_EPS) * w
        return (xn * cos + pltpu.roll(xn, LANES - 16, 1) * sa + pltpu.roll(xn, 16, 1) * sb)

    q = proj(_C_Q, _C_Q + ATTN_W)
    scale = HEAD_DIM ** -0.5 * math.log2(math.e)
    for j in range(ATTN_W // LANES):
        qr = norm_rope(q[:, j * LANES:(j + 1) * LANES], qnw_ref[...] * scale)
        qs = pltpu.roll(qr, HEAD_DIM, 1)
        g = (2 * j) // (N_Q_HEADS // N_KV_HEADS)
        keep = lo_half if g == 0 else jnp.logical_not(lo_half)
        h0 = qr if g == 0 else qs
        h1 = qs if g == 0 else qr
        q_ref[:, (2 * j) * LANES:(2 * j + 1) * LANES] = jnp.where(keep, h0, 0.0).astype(BF16)
        q_ref[:, (2 * j + 1) * LANES:(2 * j + 2) * LANES] = jnp.where(keep, h1, 0.0).astype(BF16)

    kv = proj(_C_KV, _C_KV + 2 * KV_W)
    k_ref[...] = norm_rope(kv[:, :KV_W], knw_ref[...]).astype(BF16)
    v_ref[...] = kv[:, KV_W:].astype(BF16)
    dn_ref[...] = proj(_C_DN, _C_DN + 3 * DN_W).astype(BF16)
    z_ref[...] = proj(_C_Z, _C_Z + DN_W).astype(BF16)
    ba_ref[...] = proj(_C_BA, _C_BA + LANES)[:, :4 * N_DN_HEADS]


def _inproj(x2d, T, norm1_w, w_in_b, q_norm_w, k_norm_w, tm=512):
    n = x2d.shape[0]
    cos, sa, sb = _rope_tables(T)
    tm = min(tm, T)
    tpb = T // tm
    tab_spec = pl.BlockSpec((tm, LANES), lambda i: (i % tpb, 0))
    const = lambda shape: pl.BlockSpec(shape, lambda i: (0,) * len(shape))
    row = lambda w: pl.BlockSpec((tm, w), lambda i: (i, 0))
    qnw = jnp.tile(q_norm_w, LANES // HEAD_DIM)[None, :]
    knw = jnp.tile(k_norm_w, LANES // HEAD_DIM)[None, :]
    return pl.pallas_call(
        _inproj_kernel,
        grid=(n // tm,),
        in_specs=[row(D_MODEL), const((1, D_MODEL)), const((D_MODEL, PROJ_PAD)),
                  const((1, LANES)), const((1, LANES)), tab_spec, tab_spec, tab_spec,
                  const((LANES, LANES))],
        out_specs=[row(N_Q_HEADS * LANES), row(KV_W), row(KV_W), row(3 * DN_W), row(DN_W),
                   row(4 * N_DN_HEADS)],
        out_shape=[jax.ShapeDtypeStruct((n, N_Q_HEADS * LANES), BF16),
                   jax.ShapeDtypeStruct((n, KV_W), BF16),
                   jax.ShapeDtypeStruct((n, KV_W), BF16),
                   jax.ShapeDtypeStruct((n, 3 * DN_W), BF16),
                   jax.ShapeDtypeStruct((n, DN_W), BF16),
                   jax.ShapeDtypeStruct((n, 4 * N_DN_HEADS), F32)],
        compiler_params=_cparams(("parallel",)),
        name="inproj",
    )(x2d, norm1_w[None, :], w_in_b, qnw, knw, cos, sa, sb, _head_mean_matrix())


_ATTN_PARTS = 4


def _attn_kernel(q_ref, k_ref, v_ref, o_ref, m_sc, acc_sc, *, tq, tk, n_kc):
    hp = N_Q_HEADS // _ATTN_PARTS
    rows = hp * tq
    qs = [jnp.concatenate([q_ref[0, :, h * LANES:(h + 1) * LANES] for h in range(u * hp, (u + 1) * hp)],
                          axis=0) for u in range(_ATTN_PARTS)]
    m_sc[...] = jnp.full(m_sc.shape, -jnp.inf, F32)
    acc_sc[...] = jnp.zeros(acc_sc.shape, F32)
    ones = jnp.ones((tk, LANES), BF16)

    def scores(start, u):
        kc = k_ref[0, pl.ds(start, tk), :]
        return lax.dot_general(qs[u], kc, (((1,), (1,)), ((), ())), preferred_element_type=F32)

    def softmax_pv(s, start, u):
        r = slice(u * rows, (u + 1) * rows)
        m_prev = m_sc[r, :]
        m_next = jnp.maximum(m_prev, jnp.max(s, axis=1, keepdims=True))
        alpha = jnp.exp2(m_prev - m_next)
        p = jnp.concatenate(
            [jnp.exp2(s[:, j * LANES:(j + 1) * LANES] - m_next) for j in range(tk // LANES)], axis=1)
        v_aug = jnp.concatenate([v_ref[0, pl.ds(start, tk), :], ones], axis=1)
        pv = jnp.dot(p.astype(BF16), v_aug, preferred_element_type=F32)
        acc_sc[r, :] = acc_sc[r, :] * jnp.concatenate([alpha, alpha], axis=1) + pv
        m_sc[r, :] = m_next

    unroll = 4 if n_kc % 4 == 0 else (2 if n_kc % 2 == 0 else 1)

    def body(c, carry):
        units = [(pl.multiple_of((c * unroll + cc) * tk, tk), u)
                 for cc in range(unroll) for u in range(_ATTN_PARTS)]
        s = {i: scores(*units[i]) for i in range(min(2, len(units)))}
        for i, (start, u) in enumerate(units):
            softmax_pv(s.pop(i), start, u)
            if i + 2 < len(units):
                s[i + 2] = scores(*units[i + 2])
        return carry

    lax.fori_loop(0, n_kc // unroll, body, 0)

    lane = lax.broadcasted_iota(jnp.int32, (1, LANES), 1)
    lo_half = lane < HEAD_DIM
    outs = []
    for h in range(N_Q_HEADS):
        a = acc_sc[h * tq:(h + 1) * tq, :]
        outs.append(a[:, :LANES] / a[:, LANES:])
    for j in range(N_Q_HEADS // 2):
        g = (2 * j) // (N_Q_HEADS // N_KV_HEADS)
        a, b = outs[2 * j], outs[2 * j + 1]
        if g == 0:
            b = pltpu.roll(b, HEAD_DIM, 1)
        else:
            a = pltpu.roll(a, HEAD_DIM, 1)
        o_ref[0, :, j * LANES:(j + 1) * LANES] = jnp.where(lo_half, a, b).astype(BF16)


def _attention(q, k, v, tq=128, tk=512):
    B, T, _ = q.shape
    tk = min(tk, T)
    kern = functools.partial(_attn_kernel, tq=tq, tk=tk, n_kc=T // tk)
    return pl.pallas_call(
        kern,
        grid=(B, T // tq),
        in_specs=[pl.BlockSpec((1, tq, N_Q_HEADS * LANES), lambda b, i: (b, i, 0)),
                  pl.BlockSpec((1, T, KV_W), lambda b, i: (b, 0, 0)),
                  pl.BlockSpec((1, T, KV_W), lambda b, i: (b, 0, 0))],
        out_specs=pl.BlockSpec((1, tq, ATTN_W), lambda b, i: (b, i, 0)),
        out_shape=jax.ShapeDtypeStruct((B, T, ATTN_W), BF16),
        scratch_shapes=[pltpu.VMEM((N_Q_HEADS * tq, LANES), F32),
                        pltpu.VMEM((N_Q_HEADS * tq, 2 * LANES), F32)],
        compiler_params=_cparams(("parallel", "parallel")),
        name="attention",
    )(q, k, v)


_GROUP = 8


def _dn_masks():
    r = lax.broadcasted_iota(jnp.int32, (CHUNK, CHUNK), 0)
    c = lax.broadcasted_iota(jnp.int32, (CHUNK, CHUNK), 1)
    return r, c


def _dn_kernel(xq_ref, xk_ref, xv_ref, z_ref, ba_ref, cw_ref, ad_ref, nw_ref, o_ref,
               xpad, qn_sc, kn_sc, vn_sc, u_sc, wq_sc, w2_sc, b_sc, qk_sc, gt_sc, s_sc, of_sc, ob_sc,
               *, T):
    n_chunks = T // CHUNK
    n_groups = n_chunks // _GROUP
    gsz = _GROUP * CHUNK
    h = pl.program_id(1)

    def conv_pass(x_ref, w_idx, dst, mode):
        xpad[0:8, :] = jnp.zeros((8, LANES), F32)
        xpad[T + 8:T + 16, :] = jnp.zeros((8, LANES), F32)
        blk = 256

        def cp(i, c):
            s = pl.multiple_of(i * blk, blk)
            xpad[pl.ds(s + 8, blk), :] = x_ref[0, pl.ds(s, blk), :].astype(F32)
            return c
        lax.fori_loop(0, T // blk, cp, 0)
        w = cw_ref[w_idx]

        def cv(i, c):
            s = pl.multiple_of(i * blk, blk)
            acc = None
            for j in range(CONV_W):
                term = xpad[pl.ds(s + 8 + j - CONV_W // 2, blk), :] * w[j:j + 1, :]
                acc = term if acc is None else acc + term
            y = acc / (1.0 + jnp.exp(-acc))
            if mode == "q":
                y = y * lax.rsqrt(jnp.sum(y * y, axis=-1, keepdims=True) + NORM_EPS) * (DN_HEAD_DIM ** -0.5)
            elif mode == "k":
                y = y * lax.rsqrt(jnp.sum(y * y, axis=-1, keepdims=True) + NORM_EPS)
            dst[pl.ds(s, blk), :] = y
            return c
        lax.fori_loop(0, T // blk, cv, 0)

    conv_pass(xq_ref, 0, qn_sc, "q")
    conv_pass(xk_ref, 1, kn_sc, "k")
    conv_pass(xv_ref, 2, vn_sc, "v")

    r, c = _dn_masks()
    cum_ops = [jnp.concatenate([(c <= r).astype(BF16), (c > r).astype(BF16)], axis=0),
               jnp.concatenate([(c >= r).astype(BF16), (c < r).astype(BF16)], axis=0)]
    r2 = lax.broadcasted_iota(jnp.int32, (CHUNK, LANES), 0)
    c2 = lax.broadcasted_iota(jnp.int32, (CHUNK, LANES), 1)
    lo = c2 < CHUNK
    cc = jnp.where(lo, c2, c2 - CHUNK)
    gs_zero = [jnp.logical_and(lo, r2 <= cc), jnp.logical_and(jnp.logical_not(lo), r2 >= cc)]
    strict_p = jnp.where(lo, r2 - cc, cc - r2) > 0
    incl_p = jnp.where(lo, r2 - cc, cc - r2) >= 0
    eye_p = (cc == r2).astype(F32)
    zeros_c = jnp.zeros((CHUNK, LANES), F32)
    nt = (((1,), (1,)), ((), ()))
    tn = (((0,), (0,)), ((), ()))
    top, bot = slice(0, CHUNK), slice(CHUNK, 2 * CHUNK)
    dirs = range(2)

    def block_diag(p):
        return jnp.concatenate([jnp.where(lo, p, 0.0), jnp.where(lo, 0.0, p)], axis=0).astype(BF16)

    def gates(tok0, d):
        ba = ba_ref[0, 0, pl.ds(tok0, CHUNK), :]
        beta = 1.0 / (1.0 + jnp.exp(-ba[:, d:d + 1]))
        a_in = ba[:, 2 + d:3 + d] + ad_ref[0, 2 + d:3 + d, :][:, 0:1]
        sp = jnp.maximum(a_in, 0.0) + jnp.log(1.0 + jnp.exp(-jnp.abs(a_in)))
        return beta, -jnp.exp(ad_ref[0, d:d + 1, :][:, 0:1]) * sp

    def group_toks(gi):
        hint = (lambda x, m: x) if isinstance(gi, int) else pl.multiple_of
        base_f = hint(gi * gsz, gsz)
        base_b = hint((n_groups - 1 - gi) * gsz, gsz)
        return [(hint(base_f + j * CHUNK, CHUNK), hint(base_b + (_GROUP - 1 - j) * CHUNK, CHUNK))
                for j in range(_GROUP)]

    def prep_group(toks, buf):
        n = len(toks)
        q = [[qn_sc[pl.ds(t[d], CHUNK), :] for d in dirs] for t in toks]
        k = [[kn_sc[pl.ds(t[d], CHUNK), :] for d in dirs] for t in toks]
        v = [[vn_sc[pl.ds(t[d], CHUNK), :] for d in dirs] for t in toks]
        bg = [[gates(t[d], d) for d in dirs] for t in toks]
        cs = []
        for j in range(n):
            row = []
            for d in dirs:
                rhs = jnp.where(gs_zero[d], 0.0, jnp.broadcast_to(bg[j][d][1], (CHUNK, LANES)))
                m = jnp.dot(cum_ops[d], jnp.concatenate(_split_bf16(rhs), axis=1),
                            preferred_element_type=F32)
                row.append(m[:, :LANES] + m[:, LANES:])
            cs.append(row)
        yield
        own = [CHUNK, 0]
        gc = [[cs[j][d][top, own[d]:own[d] + 1] for d in dirs] for j in range(n)]
        grem = [[cs[j][d][bot, own[d]:own[d] + 1] for d in dirs] for j in range(n)]
        decay = [jnp.exp(jnp.where(lo, cs[j][0][top], cs[j][1][top])) for j in range(n)]
        egc = [[jnp.exp(gc[j][d]) for d in dirs] for j in range(n)]
        kb = [[k[j][d] * bg[j][d][0] for d in dirs] for j in range(n)]
        kq = []
        for j in range(n):
            kcat = jnp.concatenate([k[j][0], k[j][1]], axis=0).astype(BF16)
            kq.append([lax.dot_general(jnp.concatenate([kb[j][d], q[j][d]], axis=0).astype(BF16), kcat,
                                       nt, preferred_element_type=F32) for d in dirs])
        yield
        a = [jnp.where(strict_p, jnp.where(lo, kq[j][0][top], kq[j][1][top]) * decay[j], 0.0)
             for j in range(n)]
        for j in range(n):
            qk = jnp.where(incl_p, jnp.where(lo, kq[j][0][bot], kq[j][1][bot]) * decay[j], 0.0)
            qk_sc[buf, pl.ds(j * CHUNK, CHUNK), :] = qk.astype(BF16)
        x = [eye_p - a[j] for j in range(n)]
        p = a
        pbd = [block_diag(p[j]) for j in range(n)]
        for _ in range(5):
            p = [jnp.dot(p[j].astype(BF16), pbd[j], preferred_element_type=F32) for j in range(n)]
            yield
            pbd = [block_diag(p[j]) for j in range(n)]
            x = [x[j] + jnp.dot(x[j].astype(BF16), pbd[j], preferred_element_type=F32) for j in range(n)]
            yield
        uw = []
        for j in range(n):
            r_f = jnp.concatenate([v[j][0] * bg[j][0][0], kb[j][0] * egc[j][0], zeros_c, zeros_c], axis=1)
            r_b = jnp.concatenate([zeros_c, zeros_c, v[j][1] * bg[j][1][0], kb[j][1] * egc[j][1]], axis=1)
            uw.append(jnp.dot(x[j].astype(BF16), jnp.concatenate([r_f, r_b], axis=0).astype(BF16),
                              preferred_element_type=F32))
        yield
        for j in range(n):
            for d in dirs:
                u = uw[j][:, 2 * d * LANES:(2 * d + 1) * LANES]
                w = uw[j][:, (2 * d + 1) * LANES:(2 * d + 2) * LANES]
                kd = (k[j][d] * jnp.exp(grem[j][d])).astype(BF16)
                w2b = lax.dot_general(kd, jnp.concatenate([w, u], axis=1).astype(BF16), tn,
                                      preferred_element_type=F32)
                blk = pl.ds(j * 2 * CHUNK, 2 * CHUNK)
                bd = buf * 2 + d
                u_sc[bd, pl.ds(j * CHUNK, CHUNK), :] = u
                wq_sc[bd, blk, :] = jnp.concatenate([w, q[j][d] * egc[j][d]], axis=0).astype(BF16)
                w2_sc[bd, blk, :] = w2b[:, :LANES].astype(BF16)
                b_sc[bd, blk, :] = w2b[:, LANES:]
                tot = gc[j][d][CHUNK - 1:CHUNK, :] if d == 0 else gc[j][d][0:1, :]
                gt_sc[bd, pl.ds(j * 8, 8), :] = jnp.broadcast_to(jnp.exp(tot), (8, LANES))
        yield

    def recur_group(buf, toks):
        s = [s_sc[0], s_sc[1]]
        for j in range(_GROUP):
            blk = pl.ds(j * 2 * CHUNK, 2 * CHUNK)
            ws, sw = [], []
            for d in dirs:
                sb = s[d].astype(BF16)
                ws.append(jnp.dot(w2_sc[buf * 2 + d, blk, :], sb, preferred_element_type=F32))
                sw.append(jnp.dot(wq_sc[buf * 2 + d, blk, :], sb, preferred_element_type=F32))
                yield
            s = [s[d] * gt_sc[buf * 2 + d, pl.ds(j * 8, 1), :] + b_sc[buf * 2 + d, blk, :] - ws[d] for d in dirs]
            vn = [u_sc[buf * 2 + d, pl.ds(j * CHUNK, CHUNK), :] - sw[d][top] for d in dirs]
            vbd = jnp.concatenate([jnp.concatenate([vn[0], zeros_c], axis=1),
                                   jnp.concatenate([zeros_c, vn[1]], axis=1)], axis=0).astype(BF16)
            oo = jnp.dot(qk_sc[buf, pl.ds(j * CHUNK, CHUNK), :], vbd, preferred_element_type=F32)
            of_sc[pl.ds(toks[j][0], CHUNK), :] = sw[0][bot] + oo[:, :LANES]
            ob_sc[pl.ds(toks[j][1], CHUNK), :] = sw[1][bot] + oo[:, LANES:]
            yield
        s_sc[0] = s[0]
        s_sc[1] = s[1]

    n_stage = 14
    n_piece = 3 * _GROUP

    for _ in prep_group(group_toks(0), 0):
        pass
    s_sc[...] = jnp.zeros(s_sc.shape, F32)

    def group_body(gi, carry):
        buf = gi % 2
        nxt = prep_group(group_toks(gi + 1), 1 - buf)
        cur = recur_group(buf, group_toks(gi))
        done = 0
        for st in range(n_stage):
            next(nxt)
            want = -(-n_piece * (st + 1) // n_stage)
            for _ in range(want - done):
                next(cur)
            done = want
        for _ in cur:
            pass
        return carry

    lax.fori_loop(0, n_groups - 1, group_body, 0)
    for _ in recur_group((n_groups - 1) % 2, group_toks(n_groups - 1)):
        pass

    blk = 256

    def fin(i, carry):
        s = pl.multiple_of(i * blk, blk)
        o = of_sc[pl.ds(s, blk), :] + ob_sc[pl.ds(s, blk), :]
        o = o * lax.rsqrt(jnp.mean(o * o, axis=-1, keepdims=True) + NORM_EPS) * nw_ref[...]
        z = z_ref[0, pl.ds(s, blk), :].astype(F32)
        o_ref[0, pl.ds(s, blk), :] = (o * (z / (1.0 + jnp.exp(-z)))).astype(BF16)
        return carry
    lax.fori_loop(0, T // blk, fin, 0)


def _deltanet(dn, z, ba, conv_w, A_log, dt_bias, dn_norm_w):
    B, T, _ = dn.shape
    H = N_DN_HEADS
    ba_t = ba.reshape(B, T, 2, 2, H).transpose(0, 4, 1, 2, 3).reshape(B, H, T, 4)
    cw = jnp.pad(conv_w, ((0, 8 - CONV_W), (0, 0))).reshape(8, 3, H, DN_HEAD_DIM).transpose(2, 1, 0, 3)
    cw = cw.reshape(H * 3, 8, DN_HEAD_DIM)
    ad = jnp.concatenate([A_log, dt_bias], axis=0).T
    ad = jnp.broadcast_to(ad[:, :, None], (H, 4, LANES))
    ad = jnp.pad(ad, ((0, 0), (0, 4), (0, 0)))
    gsz = _GROUP * CHUNK
    kern = functools.partial(_dn_kernel, T=T)
    col = lambda off: pl.BlockSpec((1, T, DN_HEAD_DIM), lambda b, h, off=off: (b, 0, off + h))
    return pl.pallas_call(
        kern,
        grid=(B, H),
        in_specs=[col(0), col(H), col(2 * H),
                  pl.BlockSpec((1, T, DN_HEAD_DIM), lambda b, h: (b, 0, h)),
                  pl.BlockSpec((1, 1, T, 4), lambda b, h: (b, h, 0, 0)),
                  pl.BlockSpec((3, 8, DN_HEAD_DIM), lambda b, h: (h, 0, 0)),
                  pl.BlockSpec((1, 8, LANES), lambda b, h: (h, 0, 0)),
                  pl.BlockSpec((1, DN_HEAD_DIM), lambda b, h: (0, 0))],
        out_specs=pl.BlockSpec((1, T, DN_HEAD_DIM), lambda b, h: (b, 0, h)),
        out_shape=jax.ShapeDtypeStruct((B, T, DN_W), BF16),
        scratch_shapes=[pltpu.VMEM((T + 16, LANES), F32),
                        pltpu.VMEM((T, LANES), F32),
                        pltpu.VMEM((T, LANES), F32),
                        pltpu.VMEM((T, LANES), F32),
                        pltpu.VMEM((4, gsz, LANES), F32),
                        pltpu.VMEM((4, 2 * gsz, LANES), BF16),
                        pltpu.VMEM((4, 2 * gsz, LANES), BF16),
                        pltpu.VMEM((4, 2 * gsz, LANES), F32),
                        pltpu.VMEM((2, gsz, LANES), BF16),
                        pltpu.VMEM((4, _GROUP * 8, LANES), F32),
                        pltpu.VMEM((2, DN_HEAD_DIM, DN_HEAD_DIM), F32),
                        pltpu.VMEM((T, LANES), F32),
                        pltpu.VMEM((T, LANES), F32)],
        compiler_params=_cparams(("parallel", "parallel")),
        name="deltanet",
    )(dn, dn, dn, z, ba_t, cw, ad, dn_norm_w[None, :])


def _outproj_kernel(x_ref, at_ref, dn_ref, wo_ref, n2w_ref, rw_ref, rb_ref,
                    h_ref, h2_ref, idx_ref, gate_ref):
    mix = (jnp.dot(at_ref[...], wo_ref[0:ATTN_W, :], preferred_element_type=F32)
           + jnp.dot(dn_ref[...], wo_ref[ATTN_W:, :], preferred_element_type=F32))
    hres = x_ref[...] + mix
    h_ref[...] = hres
    h2 = hres * lax.rsqrt(jnp.mean(hres * hres, axis=-1, keepdims=True) + NORM_EPS) * n2w_ref[...]
    h2_ref[...] = h2
    h2_hi, h2_lo = _split_bf16(h2)
    hh = jnp.dot(h2_hi, rw_ref[...], preferred_element_type=F32)
    logits = (hh[:, :LANES] + hh[:, LANES:] + jnp.dot(h2_lo, rw_ref[:, :LANES], preferred_element_type=F32)
              + rb_ref[...])
    lane = lax.broadcasted_iota(jnp.int32, logits.shape, 1).astype(F32)
    vals = []
    idxs = []
    l = logits
    for _ in range(TOP_K):
        m = jnp.max(l, axis=-1, keepdims=True)
        i = jnp.min(jnp.where(l == m, lane, float(LANES)), axis=-1, keepdims=True)
        vals.append(m)
        idxs.append(i)
        l = jnp.where(lane == i, -jnp.inf, l)
    es = [jnp.exp(v - vals[0]) for v in vals]
    den = es[0] + es[1] + es[2] + es[3]
    idx_ref[...] = jnp.concatenate(idxs, axis=1).astype(jnp.int32)
    gate_ref[...] = jnp.concatenate([e / den for e in es], axis=1)


def _outproj_router(x2d, attn2d, dn2d, w_out_b, norm2_w, router_w, router_b, tm=512):
    n = x2d.shape[0]
    rw = jnp.concatenate(_split_bf16(jnp.pad(router_w, ((0, 0), (0, LANES - N_EXPERTS)))), axis=1)
    rb = jnp.pad(router_b, (0, LANES - N_EXPERTS), constant_values=-jnp.inf)[None, :]
    const = lambda shape: pl.BlockSpec(shape, lambda i: (0,) * len(shape))
    row = lambda w: pl.BlockSpec((tm, w), lambda i: (i, 0))
    return pl.pallas_call(
        _outproj_kernel,
        grid=(n // tm,),
        in_specs=[row(D_MODEL), row(ATTN_W), row(DN_W), const((D_MODEL, D_MODEL)),
                  const((1, D_MODEL)), const((D_MODEL, 2 * LANES)), const((1, LANES))],
        out_specs=[row(D_MODEL), row(D_MODEL), row(TOP_K), row(TOP_K)],
        out_shape=[jax.ShapeDtypeStruct((n, D_MODEL), F32),
                   jax.ShapeDtypeStruct((n, D_MODEL), F32),
                   jax.ShapeDtypeStruct((n, TOP_K), jnp.int32),
                   jax.ShapeDtypeStruct((n, TOP_K), F32)],
        compiler_params=_cparams(("parallel",)),
        name="outproj_router",
    )(x2d, attn2d, dn2d, w_out_b, norm2_w[None, :], rw, rb)


_BM = 256


def _moe_kernel(be_ref, nu_ref, tok_ref, tokn_ref, dstp_ref, dstc_ref, h2_hbm, w1_ref, b1_ref, w2_ref,
                b2_ref, yk_hbm, xbuf, ybuf, act_sc, gsem, ssem, *, n_assign):
    i = pl.program_id(0)
    n_used = nu_ref[0]
    nc = 256

    def gather_row(idx_ref, sl, rr):
        return pltpu.make_async_copy(h2_hbm.at[pl.ds(idx_ref[0, 0, rr], 1)], xbuf.at[sl, pl.ds(rr, 1)],
                                     gsem.at[sl])

    def scatter_row(dst_ref, sl, rr):
        return pltpu.make_async_copy(ybuf.at[sl, pl.ds(rr, 1)], yk_hbm.at[pl.ds(dst_ref[0, 0, rr], 1)],
                                     ssem.at[sl])

    def wait_gather(sl):
        pltpu.make_async_copy(h2_hbm.at[pl.ds(0, _BM)], xbuf.at[sl], gsem.at[sl]).wait()

    def wait_scatter(sl):
        pltpu.make_async_copy(ybuf.at[sl], yk_hbm.at[pl.ds(0, _BM)], ssem.at[sl]).wait()

    @pl.when(i == 0)
    def _():
        ybuf[...] = jnp.zeros(ybuf.shape, F32)
        pltpu.make_async_copy(ybuf.at[0], yk_hbm.at[pl.ds(n_assign, _BM)], ssem.at[0]).start()

        def body(rr, c):
            gather_row(tok_ref, 0, rr).start()
            return c
        lax.fori_loop(0, _BM, body, 0, unroll=8)

    def step(slot, other):
        wait_gather(slot)
        jobs = ([functools.partial(gather_row, tokn_ref, other, rr) for rr in range(_BM)]
                + [functools.partial(scatter_row, dstp_ref, other, rr) for rr in range(_BM)])
        n_gaps = D_FF // nc + D_MODEL // nc
        per_gap = -(-len(jobs) // n_gaps)

        def issue(gap):
            for job in jobs[gap * per_gap:(gap + 1) * per_gap]:
                job().start()

        x = xbuf[slot].astype(BF16)
        for j in range(D_FF // nc):
            cg = slice(j * nc, (j + 1) * nc)
            cu = slice(D_FF + j * nc, D_FF + (j + 1) * nc)
            gate = jnp.dot(x, w1_ref[0, :, cg].astype(BF16), preferred_element_type=F32) + b1_ref[0, :, cg]
            up = jnp.dot(x, w1_ref[0, :, cu].astype(BF16), preferred_element_type=F32) + b1_ref[0, :, cu]
            gate = jnp.minimum(gate, SWIGLU_LIMIT)
            up = jnp.clip(up, -SWIGLU_LIMIT, SWIGLU_LIMIT)
            act = gate * (1.0 / (1.0 + jnp.exp(-SWIGLU_ALPHA * gate))) * (up + 1.0)
            act_sc[:, cg] = act.astype(BF16)
            issue(j)
        wait_scatter(slot)
        a = act_sc[...]
        for j in range(D_MODEL // nc):
            cn = slice(j * nc, (j + 1) * nc)
            ybuf[slot, :, cn] = (jnp.dot(a, w2_ref[0, :, cn].astype(BF16), preferred_element_type=F32)
                                 + b2_ref[0, :, cn])
            issue(D_FF // nc + j)

        @pl.when(i == n_used - 1)
        def _():
            def body(rr, c):
                scatter_row(dstc_ref, slot, rr).start()
                return c
            lax.fori_loop(0, _BM, body, 0, unroll=8)
            wait_scatter(other)
            wait_scatter(slot)
            wait_gather(other)

    for parity in range(2):
        pl.when(jnp.logical_and(i < n_used, i % 2 == parity))(functools.partial(step, parity, 1 - parity))


def _moe_blocks(n_tok):
    n_assign = n_tok * TOP_K
    return n_assign // _BM + N_EXPERTS


def _routing_plan(idx):
    n_tok = idx.shape[0]
    n_assign = n_tok * TOP_K
    nb = _moe_blocks(n_tok)
    flat_e = idx.reshape(-1)
    order = jnp.argsort(flat_e, stable=True).astype(jnp.int32)
    counts = jnp.sum(flat_e[:, None] == jnp.arange(N_EXPERTS, dtype=jnp.int32)[None, :], axis=0,
                     dtype=jnp.int32)
    padded = (counts + _BM - 1) // _BM * _BM
    start = jnp.cumsum(counts) - counts
    pad_end = jnp.cumsum(padded)
    pad_start = pad_end - padded
    n_used = (pad_end[-1] // _BM).astype(jnp.int32)
    bidx = jnp.arange(nb, dtype=jnp.int32)
    blk = jnp.minimum(bidx, n_used - 1) * _BM
    block_e = jnp.minimum(jnp.sum(pad_end[None, :] <= blk[:, None], axis=1), N_EXPERTS - 1).astype(jnp.int32)
    lane = jnp.arange(_BM, dtype=jnp.int32)[None, :]
    rank = blk[:, None] - pad_start[block_e][:, None] + lane
    valid = rank < counts[block_e][:, None]
    src = jnp.clip(start[block_e][:, None] + rank, 0, n_assign - 1)
    assign = jnp.where(valid, order[src], 0)
    row_tok = assign // TOP_K
    dump = n_assign + (bidx[:, None] % 2) * _BM + lane
    row_dst = jnp.where(valid, (assign % TOP_K) * n_tok + row_tok, dump)
    row_dst = jnp.concatenate([n_assign + _BM + lane, row_dst], axis=0)
    return (block_e, n_used.reshape(1), row_tok.reshape(nb, 1, _BM), row_dst.reshape(nb + 1, 1, _BM))


def _moe(h2, idx, w1_b, b1, w2_b, b2):
    n_tok = h2.shape[0]
    n_assign = n_tok * TOP_K
    nb = _moe_blocks(n_tok)
    block_e, n_used, row_tok, row_dst = _routing_plan(idx)
    smem_blk = lambda f: pl.BlockSpec((1, 1, _BM), f, memory_space=pltpu.SMEM)
    grid_spec = pltpu.PrefetchScalarGridSpec(
        num_scalar_prefetch=2,
        grid=(nb,),
        in_specs=[smem_blk(lambda i, be, nu: (i, 0, 0)),
                  smem_blk(lambda i, be, nu: (jnp.minimum(i + 1, nb - 1), 0, 0)),
                  smem_blk(lambda i, be, nu: (i, 0, 0)),
                  smem_blk(lambda i, be, nu: (i + 1, 0, 0)),
                  pl.BlockSpec(memory_space=pl.ANY),
                  pl.BlockSpec((1, D_MODEL, 2 * D_FF), lambda i, be, nu: (be[i], 0, 0)),
                  pl.BlockSpec((1, 1, 2 * D_FF), lambda i, be, nu: (be[i], 0, 0)),
                  pl.BlockSpec((1, D_FF, D_MODEL), lambda i, be, nu: (be[i], 0, 0)),
                  pl.BlockSpec((1, 1, D_MODEL), lambda i, be, nu: (be[i], 0, 0))],
        out_specs=pl.BlockSpec(memory_space=pl.ANY),
        scratch_shapes=[pltpu.VMEM((2, _BM, D_MODEL), F32),
                        pltpu.VMEM((2, _BM, D_MODEL), F32),
                        pltpu.VMEM((_BM, D_FF), BF16),
                        pltpu.SemaphoreType.DMA((2,)),
                        pltpu.SemaphoreType.DMA((2,))],
    )
    return pl.pallas_call(
        functools.partial(_moe_kernel, n_assign=n_assign),
        grid_spec=grid_spec,
        out_shape=jax.ShapeDtypeStruct((n_assign + 2 * _BM, D_MODEL), F32),
        compiler_params=_cparams(("arbitrary",)),
        name="moe_experts",
    )(block_e, n_used, row_tok, row_tok, row_dst, row_dst, h2, w1_b, b1[:, None, :], w2_b, b2[:, None, :])


def _combine_kernel(h_ref, y0_ref, y1_ref, y2_ref, y3_ref, g_ref, o_ref):
    g = g_ref[...]
    acc = h_ref[...]
    for k, y_ref in enumerate((y0_ref, y1_ref, y2_ref, y3_ref)):
        acc = acc + g[:, k:k + 1] * y_ref[...]
    o_ref[...] = acc


def _combine(h, yk, gates, tm=256):
    n = h.shape[0]
    nt = n // tm
    row = lambda w: pl.BlockSpec((tm, w), lambda i: (i, 0))
    ysp = lambda k: pl.BlockSpec((tm, D_MODEL), lambda i, k=k: (k * nt + i, 0))
    return pl.pallas_call(
        _combine_kernel,
        grid=(nt,),
        in_specs=[row(D_MODEL), ysp(0), ysp(1), ysp(2), ysp(3), row(TOP_K)],
        out_specs=row(D_MODEL),
        out_shape=jax.ShapeDtypeStruct((n, D_MODEL), F32),
        compiler_params=_cparams(("parallel",)),
        name="moe_combine",
    )(h, yk, yk, yk, yk, gates)


def _layer(x, p):
    B, T, D = x.shape
    x2d = x.reshape(B * T, D)
    q, k, v, dn, z, ba = _inproj(x2d, T, p["norm1_w"], p["w_in"], p["q_norm_w"], p["k_norm_w"])
    attn = _attention(q.reshape(B, T, -1), k.reshape(B, T, -1), v.reshape(B, T, -1))
    dno = _deltanet(dn.reshape(B, T, -1), z.reshape(B, T, -1), ba.reshape(B, T, -1),
                    p["conv_w"], p["A_log"], p["dt_bias"], p["dn_norm_w"])
    h, h2, idx, gates = _outproj_router(x2d, attn.reshape(B * T, -1), dno.reshape(B * T, -1),
                                        p["w_out"], p["norm2_w"], p["router_w"], p["router_b"])
    yk = _moe(h2, idx, p["w1"], p["b1"], p["w2"], p["b2"])
    y = _combine(h, yk, gates)
    return y.reshape(B, T, D)


def kernel(x_prompt, x_sample, norm1_w, w_in, conv_w, q_norm_w, k_norm_w, A_log, dt_bias, dn_norm_w,
           w_out, norm2_w, router_w, router_b, w1, b1, w2, b2):
    l = 0
    p = {
        "norm1_w": norm1_w[l],
        "w_in": jnp.pad(w_in[l], ((0, 0), (0, PROJ_PAD - w_in.shape[-1]))).astype(BF16),
        "conv_w": conv_w[l], "q_norm_w": q_norm_w[l], "k_norm_w": k_norm_w[l],
        "A_log": A_log[l], "dt_bias": dt_bias[l], "dn_norm_w": dn_norm_w[l],
        "w_out": w_out[l].astype(BF16), "norm2_w": norm2_w[l],
        "router_w": router_w[l], "router_b": router_b[l],
        "w1": w1[l], "b1": b1[l], "w2": w2[l], "b2": b2[l],
    }
    return (_layer(x_prompt, p), _layer(x_sample, p))
```
